```python
import math
import jax, jax.numpy as jnp
from jax import lax
import numpy as np

D_MODEL = 1024
BATCH = 4
SEQ = 4096
DEPTH = 2

EXPAND = 2
MIX_W = EXPAND * D_MODEL
A_W = MIX_W // 2
B_W = MIX_W - A_W
POOL_WINDOWS = (2, 4, 8, 16)
POOL_GROUPS = len(POOL_WINDOWS)
POOL_CH = A_W // POOL_GROUPS
CONV_K = 31
IN0_W = A_W + 2 * B_W + MIX_W

HEAD_DIM = 64
N_HEADS = D_MODEL // HEAD_DIM
ATT_W = N_HEADS * HEAD_DIM
IN1_W = 4 * ATT_W
MOBA_BLOCK = 256
MOBA_TOPK = 3
Q_CHUNK = 32

N_EVEN = (DEPTH + 1) // 2
N_ODD = DEPTH // 2
EPS = 1e-6

kernel_name = "hybrid_pool_conformer_moba_block"


def rms_norm(x, g):
    xf = x.astype(jnp.float32)
    y = xf * lax.rsqrt(jnp.mean(xf * xf, axis=-1, keepdims=True) + EPS)
    return (y * g.astype(jnp.float32)).astype(x.dtype)


def layer_norm(x, g, b):
    xf = x.astype(jnp.float32)
    mu = jnp.mean(xf, axis=-1, keepdims=True)
    var = jnp.mean(jnp.square(xf - mu), axis=-1, keepdims=True)
    y = (xf - mu) * lax.rsqrt(var + EPS)
    return (y * g.astype(jnp.float32) + b.astype(jnp.float32)).astype(x.dtype)


def alibi_slopes(n_heads):
    return jnp.exp2(-8.0 * jnp.arange(1, n_heads + 1, dtype=jnp.float32) / n_heads)


def multiscale_pool(a):
    B, S, G, C = a.shape
    af = a.astype(jnp.float32)
    cs = jnp.concatenate([jnp.zeros((B, 1, G, C), jnp.float32), jnp.cumsum(af, axis=1)], axis=1)
    t = jnp.arange(S)
    outs = []
    for g, w in enumerate(POOL_WINDOWS):
        lo = jnp.maximum(t + 1 - w, 0)
        win_sum = cs[:, 1:, g] - cs[:, lo, g]
        cnt = (t + 1 - lo).astype(jnp.float32)[None, :, None]
        outs.append(win_sum / cnt - af[:, :, g])
    return jnp.stack(outs, axis=2).astype(a.dtype)


def pool_conv_layer(x, norm_g, w_in, pool_w, pool_scale, conv_w, conv_b, cn_g, cn_b, w_out):
    B, S, _ = x.shape
    h = rms_norm(x, norm_g)
    proj = jnp.einsum('bsd,de->bse', h, w_in)
    a, bv, bg, z = jnp.split(proj, [A_W, A_W + B_W, A_W + 2 * B_W], axis=-1)
    pooled = multiscale_pool(a.reshape(B, S, POOL_GROUPS, POOL_CH))
    ya = jnp.einsum('bsgc,gce->bsge', pooled, pool_w).reshape(B, S, A_W) * pool_scale
    u = bv * jax.nn.sigmoid(bg)
    u = lax.conv_general_dilated(u, conv_w[:, None, :], window_strides=(1,),
                                 padding=[(CONV_K - 1, 0)],
                                 dimension_numbers=('NWC', 'WIO', 'NWC'),
                                 feature_group_count=B_W) + conv_b
    yb = jax.nn.silu(layer_norm(u, cn_g, cn_b))
    y = jnp.concatenate([ya, yb], axis=-1) * jax.nn.silu(z)
    return jnp.einsum('bse,ed->bsd', y, w_out)


def moba_attention(q, k, v, slopes):
    B, H, S, dh = q.shape
    nb = -(-S // MOBA_BLOCK)
    s_pad = nb * MOBA_BLOCK
    pad = ((0, 0), (0, 0), (0, s_pad - S), (0, 0))
    kp = jnp.pad(k, pad)
    vp = jnp.pad(v, pad)
    kb = kp.reshape(B, H, nb, MOBA_BLOCK, dh)
    vb = vp.reshape(B, H, nb, MOBA_BLOCK, dh)
    kmean = jnp.mean(kb.astype(jnp.float32), axis=3)
    n_sel = min(MOBA_TOPK, nb)
    bi = jnp.arange(B)[:, None, None, None]
    hi = jnp.arange(H)[None, :, None, None]
    blk_off = jnp.arange(MOBA_BLOCK)
    sl = slopes[None, :, None]

    def chunk(ci):
        t0 = ci * Q_CHUNK
        qc = lax.dynamic_slice_in_dim(q, t0, Q_CHUNK, axis=2)
        tpos = t0 + jnp.arange(Q_CHUNK)
        own = t0 // MOBA_BLOCK
        gate = jnp.einsum('bhqd,bhnd->bhqn', qc.astype(jnp.float32), kmean)
        gate = jnp.where(jnp.arange(nb) < own, gate, -jnp.inf)
        _, sel = lax.top_k(gate, n_sel)
        valid = jnp.arange(n_sel) < own
        k_sel = kb[bi, hi, sel]
        v_sel = vb[bi, hi, sel]
        s_sel = jnp.einsum('bhqd,bhqnkd->bhqnk', qc, k_sel).astype(jnp.float32)
        spos = sel[..., None] * MOBA_BLOCK + blk_off
        dist_sel = (tpos[None, None, :, None, None] - spos).astype(jnp.float32)
        s_sel = s_sel - sl[..., None, None] * dist_sel
        s_sel = jnp.where(valid[None, None, None, :, None], s_sel, -jnp.inf)
        s_sel = s_sel.reshape(B, H, Q_CHUNK, n_sel * MOBA_BLOCK)
        k_own = lax.dynamic_slice_in_dim(kp, own * MOBA_BLOCK, MOBA_BLOCK, axis=2)
        v_own = lax.dynamic_slice_in_dim(vp, own * MOBA_BLOCK, MOBA_BLOCK, axis=2)
        s_own = jnp.einsum('bhqd,bhkd->bhqk', qc, k_own).astype(jnp.float32)
        dist_own = tpos[:, None] - (own * MOBA_BLOCK + blk_off)[None, :]
        s_own = jnp.where(dist_own >= 0, s_own - sl[..., None] * dist_own.astype(jnp.float32), -jnp.inf)
        p = jax.nn.softmax(jnp.concatenate([s_sel, s_own], axis=-1), axis=-1).astype(v.dtype)
        p_sel = p[..., : n_sel * MOBA_BLOCK].reshape(B, H, Q_CHUNK, n_sel, MOBA_BLOCK)
        p_own = p[..., n_sel * MOBA_BLOCK:]
        o = jnp.einsum('bhqnk,bhqnkd->bhqd', p_sel, v_sel) + jnp.einsum('bhqk,bhkd->bhqd', p_own, v_own)
        return o.astype(q.dtype)

    outs = lax.map(chunk, jnp.arange(S // Q_CHUNK))
    return outs.transpose(1, 2, 0, 3, 4).reshape(B, H, S, dh)


def moba_layer(x, norm_g, w_in, q_norm_g, k_norm_g, w_out):
    B, S, _ = x.shape
    h = rms_norm(x, norm_g)
    proj = jnp.einsum('bsd,de->bse', h, w_in)
    q, k, v, z = jnp.split(proj, 4, axis=-1)
    to_heads = lambda t: t.reshape(B, S, N_HEADS, HEAD_DIM).transpose(0, 2, 1, 3)
    q = rms_norm(to_heads(q), q_norm_g) * (HEAD_DIM ** -0.5)
    k = rms_norm(to_heads(k), k_norm_g)
    o = moba_attention(q, k, to_heads(v), alibi_slopes(N_HEADS))
    o = o.transpose(0, 2, 1, 3).reshape(B, S, ATT_W)
    return jnp.einsum('bse,ed->bsd', o * jax.nn.silu(z), w_out)


def setup_inputs(seed: int = 0) -> dict:
    key = jax.random.key(seed)
    ks = jax.random.split(key, 20)
    f32 = jnp.float32
    nrm = lambda k, shape, scale: jax.random.normal(k, shape, f32) * scale
    return {
        "x": nrm(ks[0], (BATCH, SEQ, D_MODEL), 1.0),
        "norm0_g": 1.0 + nrm(ks[1], (N_EVEN, D_MODEL), 0.05),
        "w_in0": nrm(ks[2], (N_EVEN, D_MODEL, IN0_W), D_MODEL ** -0.5),
        "pool_w": nrm(ks[3], (N_EVEN, POOL_GROUPS, POOL_CH, POOL_CH), POOL_CH ** -0.5),
        "pool_scale": 1.0 + nrm(ks[4], (N_EVEN, A_W), 0.1),
        "conv_w": nrm(ks[5], (N_EVEN, CONV_K, B_W), CONV_K ** -0.5),
        "conv_b": nrm(ks[6], (N_EVEN, B_W), 0.02),
        "cnorm_g": 1.0 + nrm(ks[7], (N_EVEN, B_W), 0.05),
        "cnorm_b": nrm(ks[8], (N_EVEN, B_W), 0.02),
        "w_out0": nrm(ks[9], (N_EVEN, MIX_W, D_MODEL), 0.5 * MIX_W ** -0.5),
        "norm1_g": 1.0 + nrm(ks[10], (N_ODD, D_MODEL), 0.05),
        "w_in1": nrm(ks[11], (N_ODD, D_MODEL, IN1_W), D_MODEL ** -0.5),
        "q_norm_g": 1.0 + nrm(ks[12], (N_ODD, HEAD_DIM), 0.05),
        "k_norm_g": 1.0 + nrm(ks[13], (N_ODD, HEAD_DIM), 0.05),
        "w_out1": nrm(ks[14], (N_ODD, ATT_W, D_MODEL), 0.5 * ATT_W ** -0.5),
    }


def reference(x, norm0_g, w_in0, pool_w, pool_scale, conv_w, conv_b, cnorm_g, cnorm_b, w_out0,
              norm1_g, w_in1, q_norm_g, k_norm_g, w_out1):
    for layer in range(DEPTH):
        i = layer // 2
        if layer % 2 == 0:
            x = x + pool_conv_layer(x, norm0_g[i], w_in0[i], pool_w[i], pool_scale[i], conv_w[i],
                                    conv_b[i], cnorm_g[i], cnorm_b[i], w_out0[i])
        else:
            x = x + moba_layer(x, norm1_g[i], w_in1[i], q_norm_g[i], k_norm_g[i], w_out1[i])
    return x
```

```python
import functools

import jax
import jax.numpy as jnp
from jax import lax
from jax.experimental import pallas as pl
from jax.experimental.pallas import tpu as pltpu

D_MODEL = 1024
A_W = 1024
B_W = 1024
MIX_W = 2048
POOL_WINDOWS = (2, 4, 8, 16)
POOL_CH = 256
CONV_K = 31
IN0_W = A_W + 2 * B_W + MIX_W
HEAD_DIM = 64
N_HEADS = 16
ATT_W = 1024
MOBA_BLOCK = 256
MOBA_TOPK = 3
EPS = 1e-6

SEQ_TILE = 256
A_HALO = 16
U_HALO = 32
LANES = 128
HEADS_PER_GROUP = LANES // HEAD_DIM
VMEM_LIMIT = 56 * 1024 * 1024

F32 = jnp.float32
BF16 = jnp.bfloat16


def _dot(a, b):
    return jnp.dot(a, b, preferred_element_type=F32)


def _dot_nt(a, b):
    return lax.dot_general(a, b, (((1,), (1,)), ((), ())), preferred_element_type=F32)


def _rms_norm_rows(x, g):
    ms = jnp.mean(x * x, axis=-1, keepdims=True)
    return x * lax.rsqrt(ms + EPS) * g


def _sigmoid(x):
    return 1.0 / (1.0 + jnp.exp(-x))


def _pool_conv_kernel(x_ref, ng_ref, win_ref, pw_ref, ps_ref, cw_ref, cb_ref, cg_ref, cbeta_ref,
                      wout_ref, o_ref, a_buf, u_buf):
    ts = SEQ_TILE
    i = pl.program_id(1)

    @pl.when(i == 0)
    def _():
        a_buf[0:A_HALO, :] = jnp.zeros((A_HALO, A_W), F32)
        u_buf[0:U_HALO, :] = jnp.zeros((U_HALO, B_W), F32)

    @pl.when(i > 0)
    def _():
        a_buf[0:A_HALO, :] = a_buf[ts:ts + A_HALO, :]
        u_buf[0:U_HALO, :] = u_buf[ts:ts + U_HALO, :]

    x = x_ref[0]
    h = _rms_norm_rows(x, ng_ref[...]).astype(BF16)

    a_buf[A_HALO:A_HALO + ts, :] = _dot(h, win_ref[:, 0:A_W])
    bv = _dot(h, win_ref[:, A_W:A_W + B_W])
    bg = _dot(h, win_ref[:, A_W + B_W:A_W + 2 * B_W])
    u_buf[U_HALO:U_HALO + ts, :] = bv * _sigmoid(bg)
    z = _dot(h, win_ref[:, A_W + 2 * B_W:IN0_W])
    zg = z * _sigmoid(z)

    tpos = i * ts + lax.broadcasted_iota(jnp.int32, (ts, 1), 0)
    out = x
    for g, w in enumerate(POOL_WINDOWS):
        c0 = g * POOL_CH
        cur = a_buf[A_HALO:A_HALO + ts, c0:c0 + POOL_CH]
        win = cur
        for d in range(1, w):
            win = win + a_buf[A_HALO - d:A_HALO - d + ts, c0:c0 + POOL_CH]
        cnt = jnp.minimum(tpos + 1, w).astype(F32)
        pooled = win / cnt - cur
        ya = _dot(pooled.astype(BF16), pw_ref[g]) * ps_ref[:, c0:c0 + POOL_CH]
        ya = (ya * zg[:, c0:c0 + POOL_CH]).astype(BF16)
        out = out + _dot(ya, wout_ref[c0:c0 + POOL_CH, :])

    conv = jnp.broadcast_to(cb_ref[...], (ts, B_W))
    base = U_HALO - (CONV_K - 1)
    for k in range(CONV_K):
        conv = conv + cw_ref[k:k + 1, :] * u_buf[base + k:base + k + ts, :]
    mu = jnp.mean(conv, axis=-1, keepdims=True)
    cen = conv - mu
    var = jnp.mean(cen * cen, axis=-1, keepdims=True)
    yn = cen * lax.rsqrt(var + EPS) * cg_ref[...] + cbeta_ref[...]
    yb = (yn * _sigmoid(yn) * zg[:, A_W:MIX_W]).astype(BF16)
    out = out + _dot(yb, wout_ref[A_W:MIX_W, :])
    o_ref[0] = out


def _pool_conv_layer(x, norm_g, w_in, pool_w, pool_scale, conv_w, conv_b, cn_g, cn_b, w_out):
    B, S, D = x.shape
    ts = SEQ_TILE
    const2 = lambda b, i: (0, 0)
    const3 = lambda b, i: (0, 0, 0)
    single = pl.Buffered(1)
    return pl.pallas_call(
        _pool_conv_kernel,
        grid=(B, S // ts),
        in_specs=[
            pl.BlockSpec((1, ts, D), lambda b, i: (b, i, 0)),
            pl.BlockSpec((1, D), const2),
            pl.BlockSpec((D, IN0_W), const2, pipeline_mode=single),
            pl.BlockSpec((len(POOL_WINDOWS), POOL_CH, POOL_CH), const3, pipeline_mode=single),
            pl.BlockSpec((1, A_W), const2),
            pl.BlockSpec((CONV_K, B_W), const2),
            pl.BlockSpec((1, B_W), const2),
            pl.BlockSpec((1, B_W), const2),
            pl.BlockSpec((1, B_W), const2),
            pl.BlockSpec((MIX_W, D), const2, pipeline_mode=single),
        ],
        out_specs=pl.BlockSpec((1, ts, D), lambda b, i: (b, i, 0)),
        out_shape=jax.ShapeDtypeStruct((B, S, D), F32),
        scratch_shapes=[
            pltpu.VMEM((A_HALO + ts, A_W), F32),
            pltpu.VMEM((U_HALO + ts, B_W), F32),
        ],
        compiler_params=pltpu.CompilerParams(
            dimension_semantics=("arbitrary", "arbitrary"), vmem_limit_bytes=VMEM_LIMIT),
        name="pool_conv_layer",
    )(x, norm_g.reshape(1, D), w_in.astype(BF16), pool_w.astype(BF16), pool_scale.reshape(1, A_W),
      conv_w, conv_b.reshape(1, B_W), cn_g.reshape(1, B_W), cn_b.reshape(1, B_W), w_out.astype(BF16))


def _head_rms_norm(t, g_row):
    low = lax.broadcasted_iota(jnp.int32, (1, LANES), 1) < HEAD_DIM
    cols = []
    for c in range(ATT_W // LANES):
        tc = t[:, c * LANES:(c + 1) * LANES]
        sq = tc * tc
        s_low = jnp.sum(jnp.where(low, sq, 0.0), axis=-1, keepdims=True)
        s_high = jnp.sum(jnp.where(low, 0.0, sq), axis=-1, keepdims=True)
        ms = jnp.where(low, s_low, s_high) * (1.0 / HEAD_DIM)
        cols.append(tc * lax.rsqrt(ms + EPS) * g_row[:, c * LANES:(c + 1) * LANES])
    return cols


def _moba_proj_kernel(x_ref, ng_ref, win_ref, qg_ref, kg_ref, q_ref, k_ref, vt_ref, gz_ref, km_ref):
    i = pl.program_id(1)
    x = x_ref[0]
    h = _rms_norm_rows(x, ng_ref[...]).astype(BF16)

    q = _dot(h, win_ref[:, 0:ATT_W])
    for c, qc in enumerate(_head_rms_norm(q, qg_ref[...])):
        q_ref[0, :, c * LANES:(c + 1) * LANES] = (qc * (HEAD_DIM ** -0.5)).astype(BF16)

    k = _dot(h, win_ref[:, ATT_W:2 * ATT_W])
    for c, kc in enumerate(_head_rms_norm(k, kg_ref[...])):
        k_ref[0, :, c * LANES:(c + 1) * LANES] = kc.astype(BF16)
        km_ref[0, 0, :, c * LANES:(c + 1) * LANES] = jnp.mean(kc, axis=0, keepdims=True)

    v = _dot(h, win_ref[:, 2 * ATT_W:3 * ATT_W])
    vt_ref[0, 0] = v.T.astype(BF16)

    z = _dot(h, win_ref[:, 3 * ATT_W:4 * ATT_W])
    gz_ref[0] = (z * _sigmoid(z)).astype(BF16)


def _moba_proj(x, norm_g, w_in, q_norm_g, k_norm_g):
    B, S, D = x.shape
    ts = MOBA_BLOCK
    nb = S // ts
    const2 = lambda b, i: (0, 0)
    row_blk = pl.BlockSpec((1, ts, ATT_W), lambda b, i: (b, i, 0))
    return pl.pallas_call(
        _moba_proj_kernel,
        grid=(B, nb),
        in_specs=[
            pl.BlockSpec((1, ts, D), lambda b, i: (b, i, 0)),
            pl.BlockSpec((1, D), const2),
            pl.BlockSpec((D, 4 * ATT_W), const2, pipeline_mode=pl.Buffered(1)),
            pl.BlockSpec((1, ATT_W), const2),
            pl.BlockSpec((1, ATT_W), const2),
        ],
        out_specs=[
            row_blk,
            row_blk,
            pl.BlockSpec((1, 1, ATT_W, ts), lambda b, i: (b, i, 0, 0)),
            row_blk,
            pl.BlockSpec((1, 1, 1, ATT_W), lambda b, i: (b, i, 0, 0)),
        ],
        out_shape=[
            jax.ShapeDtypeStruct((B, S, ATT_W), BF16),
            jax.ShapeDtypeStruct((B, S, ATT_W), BF16),
            jax.ShapeDtypeStruct((B, nb, ATT_W, ts), BF16),
            jax.ShapeDtypeStruct((B, S, ATT_W), BF16),
            jax.ShapeDtypeStruct((B, nb, 1, ATT_W), F32),
        ],
        compiler_params=pltpu.CompilerParams(
            dimension_semantics=("arbitrary", "arbitrary"), vmem_limit_bytes=VMEM_LIMIT),
        name="moba_proj",
    )(x, norm_g.reshape(1, D), w_in.astype(BF16),
      jnp.tile(q_norm_g, N_HEADS).reshape(1, ATT_W), jnp.tile(k_norm_g, N_HEADS).reshape(1, ATT_W))


def _moba_attn_kernel(slopes_ref, q_ref, k_ref, vt_ref, km_ref, gz_ref, o_ref,
                      selb_ref, m_ref, l_ref, acc_ref):
    blk = MOBA_BLOCK
    nb = km_ref.shape[1]
    p = pl.program_id(1)
    i = pl.program_id(2)
    q = q_ref[0]
    km = km_ref[0]
    lane = lax.broadcasted_iota(jnp.int32, (1, LANES), 1)
    dl = (lax.broadcasted_iota(jnp.int32, (blk, blk), 1)
          - lax.broadcasted_iota(jnp.int32, (blk, blk), 0))
    dlf = dl.astype(F32)
    blk_id = lax.broadcasted_iota(jnp.int32, (nb, blk), 0)
    neg_inf = jnp.float32(-jnp.inf)

    qh = []
    slope = []
    for hh in range(HEADS_PER_GROUP):
        in_head = (lane // HEAD_DIM) == hh
        qh.append(jnp.where(in_head, q, jnp.zeros_like(q)))
        slope.append(slopes_ref[p * HEADS_PER_GROUP + hh])

        kmh = jnp.where(in_head, km, 0.0)
        km_hi = kmh.astype(BF16)
        km_lo = (kmh - km_hi.astype(F32)).astype(BF16)
        gate = _dot_nt(km_hi, qh[hh]) + _dot_nt(km_lo, qh[hh])
        rank = jnp.zeros((nb, blk), jnp.int32)
        for mth in range(nb):
            gm = gate[mth:mth + 1, :]
            beats = (gm > gate) | ((gm == gate) & (mth < blk_id))
            rank = rank + jnp.where(mth < i, jnp.where(beats, 1, 0), 0)
        keep = (rank < MOBA_TOPK) & (blk_id < i)
        selb_ref[hh] = jnp.where(keep, 0.0, neg_inf)

        k_own = k_ref[0, pl.ds(pl.multiple_of(i * blk, blk), blk), :]
        s = _dot_nt(k_own, qh[hh]) - slope[hh] * dlf
        s = jnp.where(dl >= 0, s, neg_inf)
        m = jnp.max(s, axis=0, keepdims=True)
        pr = jnp.exp(s - m)
        m_ref[hh] = m
        l_ref[hh] = jnp.sum(pr, axis=0, keepdims=True)
        acc_ref[hh] = _dot(vt_ref[0, i], pr.astype(BF16))

    def past_block(j, carry):
        kj = k_ref[0, pl.ds(pl.multiple_of(j * blk, blk), blk), :]
        vtj = vt_ref[0, j]
        off = ((i - j) * blk).astype(F32)
        for hh in range(HEADS_PER_GROUP):
            s = _dot_nt(kj, qh[hh]) - slope[hh] * (dlf + off) + selb_ref[hh, pl.ds(j, 1), :]
            m_old = m_ref[hh]
            m_new = jnp.maximum(m_old, jnp.max(s, axis=0, keepdims=True))
            alpha = jnp.exp(m_old - m_new)
            pr = jnp.exp(s - m_new)
            l_ref[hh] = alpha * l_ref[hh] + jnp.sum(pr, axis=0, keepdims=True)
            acc_ref[hh] = alpha * acc_ref[hh] + _dot(vtj, pr.astype(BF16))
            m_ref[hh] = m_new
        return carry

    lax.fori_loop(0, i, past_block, 0)

    row_head = lax.broadcasted_iota(jnp.int32, (LANES, 1), 0) // HEAD_DIM
    o_t = jnp.zeros((LANES, blk), F32)
    for hh in range(HEADS_PER_GROUP):
        o_t = jnp.where(row_head == hh, acc_ref[hh] / l_ref[hh], o_t)
    o_ref[0] = (o_t.T * gz_ref[0].astype(F32)).astype(BF16)


def _moba_attn(q, k, vt, kmean, gz, slopes):
    B, S, _ = q.shape
    blk = MOBA_BLOCK
    nb = S // blk
    n_groups = ATT_W // LANES
    return pl.pallas_call(
        _moba_attn_kernel,
        grid=(B, n_groups, nb),
        in_specs=[
            pl.BlockSpec(memory_space=pltpu.SMEM),
            pl.BlockSpec((1, blk, LANES), lambda b, p, i: (b, i, p)),
            pl.BlockSpec((1, S, LANES), lambda b, p, i: (b, 0, p)),
            pl.BlockSpec((1, nb, LANES, blk), lambda b, p, i: (b, 0, p, 0)),
            pl.BlockSpec((1, nb, LANES), lambda b, p, i: (b, 0, p)),
            pl.BlockSpec((1, blk, LANES), lambda b, p, i: (b, i, p)),
        ],
        out_specs=pl.BlockSpec((1, blk, LANES), lambda b, p, i: (b, i, p)),
        out_shape=jax.ShapeDtypeStruct((B, S, ATT_W), BF16),
        scratch_shapes=[
            pltpu.VMEM((HEADS_PER_GROUP, nb, blk), F32),
            pltpu.VMEM((HEADS_PER_GROUP, 1, blk), F32),
            pltpu.VMEM((HEADS_PER_GROUP, 1, blk), F32),
            pltpu.VMEM((HEADS_PER_GROUP, LANES, blk), F32),
        ],
        compiler_params=pltpu.CompilerParams(
            dimension_semantics=("arbitrary", "arbitrary", "arbitrary"), vmem_limit_bytes=VMEM_LIMIT),
        name="moba_attn",
    )(slopes, q, k, vt, kmean, gz)


def _moba_out_kernel(x_ref, og_ref, w_ref, o_ref):
    o_ref[0] = x_ref[0] + _dot(og_ref[0], w_ref[...])


def _moba_out(x, og, w_out):
    B, S, D = x.shape
    ts = 2 * SEQ_TILE
    return pl.pallas_call(
        _moba_out_kernel,
        grid=(B, S // ts),
        in_specs=[
            pl.BlockSpec((1, ts, D), lambda b, i: (b, i, 0)),
            pl.BlockSpec((1, ts, ATT_W), lambda b, i: (b, i, 0)),
            pl.BlockSpec((ATT_W, D), lambda b, i: (0, 0), pipeline_mode=pl.Buffered(1)),
        ],
        out_specs=pl.BlockSpec((1, ts, D), lambda b, i: (b, i, 0)),
        out_shape=jax.ShapeDtypeStruct((B, S, D), F32),
        compiler_params=pltpu.CompilerParams(
            dimension_semantics=("arbitrary", "arbitrary"), vmem_limit_bytes=VMEM_LIMIT),
        name="moba_out",
    )(x, og, w_out.astype(BF16))


def kernel(x, norm0_g, w_in0, pool_w, pool_scale, conv_w, conv_b, cnorm_g, cnorm_b, w_out0,
           norm1_g, w_in1, q_norm_g, k_norm_g, w_out1):
    assert x.shape[1] % MOBA_BLOCK == 0 and x.shape[2] == D_MODEL
    x = _pool_conv_layer(x, norm0_g[0], w_in0[0], pool_w[0], pool_scale[0], conv_w[0], conv_b[0],
                         cnorm_g[0], cnorm_b[0], w_out0[0])
    q, k, vt, gz, kmean = _moba_proj(x, norm1_g[0], w_in1[0], q_norm_g[0], k_norm_g[0])
    slopes = jnp.exp2(-8.0 * jnp.arange(1, N_HEADS + 1, dtype=F32) / N_HEADS)
    og = _moba_attn(q, k, vt, kmean.reshape(kmean.shape[0], -1, ATT_W), gz, slopes)
    return _moba_out(x, og, w_out1[0])
```

```python
import math

import jax
import jax.numpy as jnp
from jax import lax
from jax.experimental import pallas as pl
from jax.experimental.pallas import tpu as pltpu

D_MODEL = 1024
A_W = 1024
B_W = 1024
MIX_W = 2048
POOL_WINDOWS = (2, 4, 8, 16)
POOL_CH = 256
CONV_K = 31
IN0_W = A_W + 2 * B_W + MIX_W
HEAD_DIM = 64
N_HEADS = 16
ATT_W = 1024
MOBA_BLOCK = 256
MOBA_TOPK = 3
EPS = 1e-6
LOG2_E = math.log2(math.e)

SEQ_TILE = 256
EW_ROWS = 32
CONV_ROWS = 64
CONV_COLS = 256
LANES = 128
SUBLANES = 8
HEADS_PER_GROUP = LANES // HEAD_DIM
GROUP = 4
VMEM_LIMIT = 56 * 1024 * 1024

F32 = jnp.float32
BF16 = jnp.bfloat16


def _dot(a, b):
    return jnp.dot(a, b, preferred_element_type=F32)


def _rms_norm_rows(x, g):
    ms = jnp.mean(x * x, axis=-1, keepdims=True)
    return x * lax.rsqrt(ms + EPS) * g


def _sigmoid(x):
    return 1.0 / (1.0 + jnp.exp(-x))


def _pool_conv_kernel(x_ref, ng_ref, win_ref, pw_ref, ps_ref, cw_ref, cb_ref, cg_ref, cbeta_ref,
                      wout_ref, o_ref, a_buf, u_buf, h_buf, hp_buf, proj_buf, pool_buf, mix_buf, y_buf):
    ts = SEQ_TILE
    strand = ts // SUBLANES
    i = pl.program_id(1)

    r0 = lax.broadcasted_iota(jnp.int32, (ts, ts), 0)
    r1 = lax.broadcasted_iota(jnp.int32, (ts, ts), 1)
    to_perm = jnp.where(r1 == (r0 % SUBLANES) * strand + r0 // SUBLANES, 1.0, 0.0).astype(BF16)
    from_perm = jnp.where(r0 == (r1 % SUBLANES) * strand + r1 // SUBLANES, 1.0, 0.0).astype(BF16)
    row = lax.broadcasted_iota(jnp.int32, (ts, 1), 0)
    first_strand = row % SUBLANES == 0
    tpos = i * ts + (row % SUBLANES) * strand + row // SUBLANES

    @pl.when(i == 0)
    def _():
        a_buf[ts:2 * ts, :] = jnp.zeros((ts, A_W), F32)
        u_buf[ts:2 * ts, :] = jnp.zeros((ts, B_W), F32)

    def last_strand_to_front(buf):
        buf[0:ts, :] = pltpu.roll(buf[ts:2 * ts, :], ts - (SUBLANES - 1), 0)

    def finish_shift(buf):
        buf[0:ts, :] = jnp.where(first_strand, buf[0:ts, :], pltpu.roll(buf[ts:2 * ts, :], 1, 0))

    def delayed(buf, d, rows, c0, width):
        start = (strand - d) * SUBLANES + rows.start
        return buf[start:start + (rows.stop - rows.start), c0:c0 + width]

    def row_chunks(n):
        return [slice(r, r + n) for r in range(0, ts, n)]

    for rows in row_chunks(EW_ROWS):
        h_buf[rows, :] = _rms_norm_rows(x_ref[0, rows, :], ng_ref[...]).astype(BF16)
    hp_buf[...] = _dot(to_perm, h_buf[...]).astype(BF16)

    last_strand_to_front(a_buf)
    a_buf[ts:2 * ts, :] = _dot(hp_buf[...], win_ref[:, 0:A_W])
    finish_shift(a_buf)
    proj_buf[...] = _dot(hp_buf[...], win_ref[:, A_W:IN0_W])

    last_strand_to_front(u_buf)
    for rows in row_chunks(EW_ROWS):
        u_buf[ts + rows.start:ts + rows.stop, :] = proj_buf[rows, 0:B_W] * _sigmoid(proj_buf[rows, B_W:2 * B_W])
    finish_shift(u_buf)

    def silu_z(rows, c0, width):
        z = proj_buf[rows, 2 * B_W + c0:2 * B_W + c0 + width]
        return z * _sigmoid(z)

    for g, w in enumerate(POOL_WINDOWS):
        c0 = g * POOL_CH
        for rows in row_chunks(CONV_ROWS):
            cur = delayed(a_buf, 0, rows, c0, POOL_CH)
            win = cur
            for d in range(1, w):
                win = win + delayed(a_buf, d, rows, c0, POOL_CH)
            cnt = jnp.minimum(tpos[rows] + 1, w).astype(F32)
            pool_buf[rows, c0:c0 + POOL_CH] = (win / cnt - cur).astype(BF16)
        mix_buf[:, c0:c0 + POOL_CH] = _dot(pool_buf[:, c0:c0 + POOL_CH], pw_ref[g])
    for rows in row_chunks(EW_ROWS):
        y_buf[rows, 0:A_W] = (mix_buf[rows, :] * ps_ref[...] * silu_z(rows, 0, A_W)).astype(BF16)

    for rows in row_chunks(CONV_ROWS):
        for c0 in range(0, B_W, CONV_COLS):
            acc = jnp.broadcast_to(cb_ref[:, c0:c0 + CONV_COLS], (CONV_ROWS, CONV_COLS))
            for k in range(CONV_K):
                acc = acc + cw_ref[k:k + 1, c0:c0 + CONV_COLS] * delayed(u_buf, CONV_K - 1 - k, rows, c0, CONV_COLS)
            mix_buf[rows, c0:c0 + CONV_COLS] = acc
    for rows in row_chunks(EW_ROWS):
        conv = mix_buf[rows, :]
        mu = jnp.mean(conv, axis=-1, keepdims=True)
        cen = conv - mu
        var = jnp.mean(cen * cen, axis=-1, keepdims=True)
        yn = cen * lax.rsqrt(var + EPS) * cg_ref[...] + cbeta_ref[...]
        y_buf[rows, A_W:MIX_W] = (yn * _sigmoid(yn) * silu_z(rows, A_W, B_W)).astype(BF16)

    y_nat = _dot(from_perm, y_buf[...]).astype(BF16)
    o_ref[0] = x_ref[0] + _dot(y_nat, wout_ref[...])


def _pool_conv_layer(x, norm_g, w_in, pool_w, pool_scale, conv_w, conv_b, cn_g, cn_b, w_out):
    B, S, D = x.shape
    ts = SEQ_TILE
    const2 = lambda b, i: (0, 0)
    const3 = lambda b, i: (0, 0, 0)
    single = pl.Buffered(1)
    return pl.pallas_call(
        _pool_conv_kernel,
        grid=(B, S // ts),
        in_specs=[
            pl.BlockSpec((1, ts, D), lambda b, i: (b, i, 0)),
            pl.BlockSpec((1, D), const2),
            pl.BlockSpec((D, IN0_W), const2, pipeline_mode=single),
            pl.BlockSpec((len(POOL_WINDOWS), POOL_CH, POOL_CH), const3, pipeline_mode=single),
            pl.BlockSpec((1, A_W), const2),
            pl.BlockSpec((CONV_K, B_W), const2),
            pl.BlockSpec((1, B_W), const2),
            pl.BlockSpec((1, B_W), const2),
            pl.BlockSpec((1, B_W), const2),
            pl.BlockSpec((MIX_W, D), const2, pipeline_mode=single),
        ],
        out_specs=pl.BlockSpec((1, ts, D), lambda b, i: (b, i, 0)),
        out_shape=jax.ShapeDtypeStruct((B, S, D), F32),
        scratch_shapes=[
            pltpu.VMEM((2 * ts, A_W), F32),
            pltpu.VMEM((2 * ts, B_W), F32),
            pltpu.VMEM((ts, D), BF16),
            pltpu.VMEM((ts, D), BF16),
            pltpu.VMEM((ts, IN0_W - A_W), F32),
            pltpu.VMEM((ts, A_W), BF16),
            pltpu.VMEM((ts, A_W), F32),
            pltpu.VMEM((ts, MIX_W), BF16),
        ],
        compiler_params=pltpu.CompilerParams(
            dimension_semantics=("arbitrary", "arbitrary"), vmem_limit_bytes=VMEM_LIMIT),
        name="pool_conv_layer",
    )(x, norm_g.reshape(1, D), w_in.astype(BF16), pool_w.astype(BF16), pool_scale.reshape(1, A_W),
      conv_w, conv_b.reshape(1, B_W), cn_g.reshape(1, B_W), cn_b.reshape(1, B_W), w_out.astype(BF16))


def _head_rms_norm(t, g_row):
    low = lax.broadcasted_iota(jnp.int32, (1, LANES), 1) < HEAD_DIM
    cols = []
    for c in range(ATT_W // LANES):
        tc = t[:, c * LANES:(c + 1) * LANES]
        sq = tc * tc
        s_low = jnp.sum(jnp.where(low, sq, 0.0), axis=-1, keepdims=True)
        s_high = jnp.sum(jnp.where(low, 0.0, sq), axis=-1, keepdims=True)
        ms = jnp.where(low, s_low, s_high) * (1.0 / HEAD_DIM)
        cols.append(tc * lax.rsqrt(ms + EPS) * g_row[:, c * LANES:(c + 1) * LANES])
    return cols


def _moba_proj_kernel(x_ref, ng_ref, win_ref, qg_ref, kg_ref, q_ref, k_ref, vt_ref, gz_ref, km_ref):
    x = x_ref[0]
    h = _rms_norm_rows(x, ng_ref[...]).astype(BF16)

    q = _dot(h, win_ref[:, 0:ATT_W])
    for c, qc in enumerate(_head_rms_norm(q, qg_ref[...])):
        q_ref[0, :, c * LANES:(c + 1) * LANES] = (qc * (HEAD_DIM ** -0.5 * LOG2_E)).astype(BF16)

    k = _dot(h, win_ref[:, ATT_W:2 * ATT_W])
    for c, kc in enumerate(_head_rms_norm(k, kg_ref[...])):
        k_ref[0, :, c * LANES:(c + 1) * LANES] = kc.astype(BF16)
        km_ref[0, 0, :, c * LANES:(c + 1) * LANES] = jnp.mean(kc, axis=0, keepdims=True)

    v = _dot(h, win_ref[:, 2 * ATT_W:3 * ATT_W])
    vt_ref[0, 0] = v.T.astype(BF16)

    z = _dot(h, win_ref[:, 3 * ATT_W:4 * ATT_W])
    gz_ref[0] = (z * _sigmoid(z)).astype(BF16)


def _moba_proj(x, norm_g, w_in, q_norm_g, k_norm_g):
    B, S, D = x.shape
    ts = MOBA_BLOCK
    nb = S // ts
    const2 = lambda b, i: (0, 0)
    row_blk = pl.BlockSpec((1, ts, ATT_W), lambda b, i: (b, i, 0))
    return pl.pallas_call(
        _moba_proj_kernel,
        grid=(B, nb),
        in_specs=[
            pl.BlockSpec((1, ts, D), lambda b, i: (b, i, 0)),
            pl.BlockSpec((1, D), const2),
            pl.BlockSpec((D, 4 * ATT_W), const2, pipeline_mode=pl.Buffered(1)),
            pl.BlockSpec((1, ATT_W), const2),
            pl.BlockSpec((1, ATT_W), const2),
        ],
        out_specs=[
            row_blk,
            row_blk,
            pl.BlockSpec((1, 1, ATT_W, ts), lambda b, i: (b, i, 0, 0)),
            row_blk,
            pl.BlockSpec((1, 1, 1, ATT_W), lambda b, i: (b, i, 0, 0)),
        ],
        out_shape=[
            jax.ShapeDtypeStruct((B, S, ATT_W), BF16),
            jax.ShapeDtypeStruct((B, S, ATT_W), BF16),
            jax.ShapeDtypeStruct((B, nb, ATT_W, ts), BF16),
            jax.ShapeDtypeStruct((B, S, ATT_W), BF16),
            jax.ShapeDtypeStruct((B, nb, 1, ATT_W), F32),
        ],
        compiler_params=pltpu.CompilerParams(
            dimension_semantics=("arbitrary", "arbitrary"), vmem_limit_bytes=VMEM_LIMIT),
        name="moba_proj",
    )(x, norm_g.reshape(1, D), w_in.astype(BF16),
      jnp.tile(q_norm_g, N_HEADS).reshape(1, ATT_W), jnp.tile(k_norm_g, N_HEADS).reshape(1, ATT_W))


def _moba_attn_kernel(slopes_ref, q_ref, k_ref, vt_ref, km_ref, gz_ref, o_ref,
                      s_ref, qa_ref, bias_ref, acc_ref, m8_ref, l8_ref):
    blk = MOBA_BLOCK
    nb = km_ref.shape[1]
    p = pl.program_id(1)
    i = pl.program_id(2)
    neg_inf = jnp.float32(-jnp.inf)

    q_t = q_ref[0].astype(F32).T
    km = km_ref[0]
    row = lax.broadcasted_iota(jnp.int32, (LANES, 1), 0)
    lane = lax.broadcasted_iota(jnp.int32, (1, LANES), 1)
    blk_id = lax.broadcasted_iota(jnp.int32, (nb, blk), 0)
    key_off = lax.broadcasted_iota(jnp.int32, (blk, LANES), 0).astype(F32)
    k_extra = jnp.where(lane < 3, key_off, 0.0).astype(BF16)
    causal = (lax.broadcasted_iota(jnp.int32, (blk, blk), 1)
              >= lax.broadcasted_iota(jnp.int32, (blk, blk), 0))

    def fold8(t, op):
        parts = [t[r:r + SUBLANES] for r in range(0, t.shape[0], SUBLANES)]
        while len(parts) > 1:
            parts = [op(parts[a], parts[a + 1]) for a in range(0, len(parts), 2)]
        return parts[0]

    q_aug = []
    cslope = []
    for hh in range(HEADS_PER_GROUP):
        cs = slopes_ref[p * HEADS_PER_GROUP + hh] * LOG2_E
        cslope.append(cs)
        cs_v = jnp.full((LANES, blk), cs, F32)
        cs_hi = cs_v.astype(BF16).astype(F32)
        cs_mid = (cs_v - cs_hi).astype(BF16).astype(F32)
        cs_lo = cs_v - cs_hi - cs_mid
        q_extra = jnp.where(row == 0, cs_hi, jnp.where(row == 1, cs_mid, jnp.where(row == 2, cs_lo, 0.0)))
        q_extra = q_extra.astype(BF16)
        q_head = jnp.where(row // HEAD_DIM == hh, q_t, 0.0).astype(BF16)
        q_aug.append(jnp.concatenate([q_head, q_extra], axis=0))

        kmh = jnp.where(lane // HEAD_DIM == hh, km, 0.0)
        km_hi = kmh.astype(BF16)
        km_lo = (kmh - km_hi.astype(F32)).astype(BF16)
        gate = _dot(km_hi, q_head) + _dot(km_lo, q_head)
        rank = jnp.zeros((nb, blk), jnp.int32)
        for mth in range(nb):
            gm = gate[mth:mth + 1, :]
            beats = (gm > gate) | ((gm == gate) & (mth < blk_id))
            rank = rank + jnp.where(mth < i, jnp.where(beats, 1, 0), 0)
        keep = (rank < MOBA_TOPK) & (blk_id < i)
        dist = ((blk_id - i) * blk).astype(F32)
        bias_ref[hh] = jnp.where(keep, cs * dist, jnp.where(blk_id == i, 0.0, neg_inf))

        qa_ref[hh] = q_aug[hh]

    def key_rows(j0, n_blocks):
        return pl.ds(pl.multiple_of(j0 * blk, blk), n_blocks * blk)

    def run_groups(n, body):
        n_full = n // GROUP

        def step(t, carry):
            body(t * GROUP, GROUP)
            return carry

        lax.fori_loop(0, n_full, step, 0)
        g = GROUP // 2
        done = n_full * GROUP
        while g >= 1:
            take = ((n - done) // g) > 0
            pl.when(take)(lambda done=done, g=g: body(done, g))
            done = done + jnp.where(take, g, 0)
            g //= 2

    k_own = jnp.concatenate([k_ref[0, key_rows(i, 1), :], k_extra], axis=1)
    for hh in range(HEADS_PER_GROUP):
        s = jnp.where(causal, _dot(k_own, qa_ref[hh]), neg_inf)
        s_ref[hh, key_rows(i, 1), :] = s
        m8_ref[hh] = fold8(s, jnp.maximum)

    def pass1(j0, n_blocks):
        k_x = jnp.concatenate([k_extra] * n_blocks, axis=0)
        kj = jnp.concatenate([k_ref[0, key_rows(j0, n_blocks), :], k_x], axis=1)
        for hh in range(HEADS_PER_GROUP):
            s = _dot(kj, qa_ref[hh])
            s_ref[hh, key_rows(j0, n_blocks), :] = s
            m = m8_ref[hh]
            for g in range(n_blocks):
                m = jnp.maximum(m, fold8(s[g * blk:(g + 1) * blk], jnp.maximum)
                                + bias_ref[hh, pl.ds(j0 + g, 1), :])
            m8_ref[hh] = m

    run_groups(i, pass1)
    m_row = [jnp.max(m8_ref[hh], axis=0, keepdims=True) for hh in range(HEADS_PER_GROUP)]

    for hh in range(HEADS_PER_GROUP):
        acc_ref[hh] = jnp.zeros((HEAD_DIM, blk), F32)
        l8_ref[hh] = jnp.zeros((SUBLANES, blk), F32)

    def pass2(j0, n_blocks):
        for hh in range(HEADS_PER_GROUP):
            l8 = l8_ref[hh]
            pv = acc_ref[hh]
            for g in range(n_blocks):
                shift = m_row[hh] - bias_ref[hh, pl.ds(j0 + g, 1), :]
                pr = jnp.exp2(s_ref[hh, key_rows(j0 + g, 1), :] - shift)
                l8 = l8 + fold8(pr, jnp.add)
                v_t = vt_ref[0, j0 + g, hh * HEAD_DIM:(hh + 1) * HEAD_DIM, :]
                pv = pv + _dot(v_t, pr.astype(BF16))
            l8_ref[hh] = l8
            acc_ref[hh] = pv

    run_groups(i + 1, pass2)
    o_t = jnp.concatenate(
        [acc_ref[hh] / jnp.sum(l8_ref[hh], axis=0, keepdims=True) for hh in range(HEADS_PER_GROUP)], axis=0)
    o_ref[0] = (o_t.T * gz_ref[0].astype(F32)).astype(BF16)


def _moba_attn(q, k, vt, kmean, gz, slopes):
    B, S, _ = q.shape
    blk = MOBA_BLOCK
    nb = S // blk
    n_groups = ATT_W // LANES
    return pl.pallas_call(
        _moba_attn_kernel,
        grid=(B, n_groups, nb),
        in_specs=[
            pl.BlockSpec(memory_space=pltpu.SMEM),
            pl.BlockSpec((1, blk, LANES), lambda b, p, i: (b, i, p)),
            pl.BlockSpec((1, S, LANES), lambda b, p, i: (b, 0, p)),
            pl.BlockSpec((1, nb, LANES, blk), lambda b, p, i: (b, 0, p, 0)),
            pl.BlockSpec((1, nb, LANES), lambda b, p, i: (b, 0, p)),
            pl.BlockSpec((1, blk, LANES), lambda b, p, i: (b, i, p)),
        ],
        out_specs=pl.BlockSpec((1, blk, LANES), lambda b, p, i: (b, i, p)),
        out_shape=jax.ShapeDtypeStruct((B, S, ATT_W), BF16),
        scratch_shapes=[
            pltpu.VMEM((HEADS_PER_GROUP, S, blk), F32),
            pltpu.VMEM((HEADS_PER_GROUP, 2 * LANES, blk), BF16),
            pltpu.VMEM((HEADS_PER_GROUP, nb, blk), F32),
            pltpu.VMEM((HEADS_PER_GROUP, HEAD_DIM, blk), F32),
            pltpu.VMEM((HEADS_PER_GROUP, SUBLANES, blk), F32),
            pltpu.VMEM((HEADS_PER_GROUP, SUBLANES, blk), F32),
        ],
        compiler_params=pltpu.CompilerParams(
            dimension_semantics=("arbitrary", "arbitrary", "arbitrary"), vmem_limit_bytes=VMEM_LIMIT),
        name="moba_attn",
    )(slopes, q, k, vt, kmean, gz)


def _moba_out_kernel(x_ref, og_ref, w_ref, o_ref):
    o_ref[0] = x_ref[0] + _dot(og_ref[0], w_ref[...])


def _moba_out(x, og, w_out):
    B, S, D = x.shape
    ts = 2 * SEQ_TILE
    return pl.pallas_call(
        _moba_out_kernel,
        grid=(B, S // ts),
        in_specs=[
            pl.BlockSpec((1, ts, D), lambda b, i: (b, i, 0)),
            pl.BlockSpec((1, ts, ATT_W), lambda b, i: (b, i, 0)),
            pl.BlockSpec((ATT_W, D), lambda b, i: (0, 0), pipeline_mode=pl.Buffered(1)),
        ],
        out_specs=pl.BlockSpec((1, ts, D), lambda b, i: (b, i, 0)),
        out_shape=jax.ShapeDtypeStruct((B, S, D), F32),
        compiler_params=pltpu.CompilerParams(
            dimension_semantics=("arbitrary", "arbitrary"), vmem_limit_bytes=VMEM_LIMIT),
        name="moba_out",
    )(x, og, w_out.astype(BF16))


def kernel(x, norm0_g, w_in0, pool_w, pool_scale, conv_w, conv_b, cnorm_g, cnorm_b, w_out0,
           norm1_g, w_in1, q_norm_g, k_norm_g, w_out1):
    assert x.shape[1] % MOBA_BLOCK == 0 and x.shape[2] == D_MODEL
    x = _pool_conv_layer(x, norm0_g[0], w_in0[0], pool_w[0], pool_scale[0], conv_w[0], conv_b[0],
                         cnorm_g[0], cnorm_b[0], w_out0[0])
    q, k, vt, gz, kmean = _moba_proj(x, norm1_g[0], w_in1[0], q_norm_g[0], k_norm_g[0])
    slopes = jnp.exp2(-8.0 * jnp.arange(1, N_HEADS + 1, dtype=F32) / N_HEADS)
    og = _moba_attn(q, k, vt, kmean.reshape(kmean.shape[0], -1, ATT_W), gz, slopes)
    return _moba_out(x, og, w_out1[0])
```

```python
import math

import jax
import jax.numpy as jnp
from jax import lax
from jax.experimental import pallas as pl
from jax.experimental.pallas import tpu as pltpu

D_MODEL = 1024
A_W = 1024
B_W = 1024
MIX_W = 2048
POOL_WINDOWS = (2, 4, 8, 16)
POOL_CH = 256
CONV_K = 31
IN0_W = A_W + 2 * B_W + MIX_W
HEAD_DIM = 64
N_HEADS = 16
ATT_W = 1024
MOBA_BLOCK = 256
MOBA_TOPK = 3
EPS = 1e-6
LOG2_E = math.log2(math.e)

SEQ_TILE = 256
EW_ROWS = 32
CONV_ROWS = 64
CONV_COLS = 256
LANES = 128
SUBLANES = 8
HEADS_PER_GROUP = LANES // HEAD_DIM
VMEM_LIMIT = 56 * 1024 * 1024

F32 = jnp.float32
BF16 = jnp.bfloat16


def _dot(a, b):
    return jnp.dot(a, b, preferred_element_type=F32)


def _rms_norm_rows(x, g):
    ms = jnp.mean(x * x, axis=-1, keepdims=True)
    return x * lax.rsqrt(ms + EPS) * g


def _sigmoid(x):
    return 1.0 / (1.0 + jnp.exp(-x))


def _pool_conv_kernel(x_ref, ng_ref, win_ref, pw_ref, ps_ref, cw_ref, cb_ref, cg_ref, cbeta_ref,
                      wout_ref, o_ref, a_buf, u_buf, h_buf, hp_buf, proj_buf, pool_buf, mix_buf, y_buf):
    ts = SEQ_TILE
    strand = ts // SUBLANES
    i = pl.program_id(1)

    r0 = lax.broadcasted_iota(jnp.int32, (ts, ts), 0)
    r1 = lax.broadcasted_iota(jnp.int32, (ts, ts), 1)
    to_perm = jnp.where(r1 == (r0 % SUBLANES) * strand + r0 // SUBLANES, 1.0, 0.0).astype(BF16)
    from_perm = jnp.where(r0 == (r1 % SUBLANES) * strand + r1 // SUBLANES, 1.0, 0.0).astype(BF16)
    row = lax.broadcasted_iota(jnp.int32, (ts, 1), 0)
    first_strand = row % SUBLANES == 0
    tpos = i * ts + (row % SUBLANES) * strand + row // SUBLANES

    @pl.when(i == 0)
    def _():
        a_buf[ts:2 * ts, :] = jnp.zeros((ts, A_W), F32)
        u_buf[ts:2 * ts, :] = jnp.zeros((ts, B_W), F32)

    def last_strand_to_front(buf):
        buf[0:ts, :] = pltpu.roll(buf[ts:2 * ts, :], ts - (SUBLANES - 1), 0)

    def finish_shift(buf):
        buf[0:ts, :] = jnp.where(first_strand, buf[0:ts, :], pltpu.roll(buf[ts:2 * ts, :], 1, 0))

    def delayed(buf, d, rows, c0, width):
        start = (strand - d) * SUBLANES + rows.start
        return buf[start:start + (rows.stop - rows.start), c0:c0 + width]

    def row_chunks(n):
        return [slice(r, r + n) for r in range(0, ts, n)]

    for rows in row_chunks(EW_ROWS):
        h_buf[rows, :] = _rms_norm_rows(x_ref[0, rows, :], ng_ref[...]).astype(BF16)
    hp_buf[...] = _dot(to_perm, h_buf[...]).astype(BF16)

    last_strand_to_front(a_buf)
    a_buf[ts:2 * ts, :] = _dot(hp_buf[...], win_ref[:, 0:A_W])
    finish_shift(a_buf)
    proj_buf[...] = _dot(hp_buf[...], win_ref[:, A_W:IN0_W])

    last_strand_to_front(u_buf)
    for rows in row_chunks(EW_ROWS):
        u_buf[ts + rows.start:ts + rows.stop, :] = proj_buf[rows, 0:B_W] * _sigmoid(proj_buf[rows, B_W:2 * B_W])
    finish_shift(u_buf)

    def silu_z(rows, c0, width):
        z = proj_buf[rows, 2 * B_W + c0:2 * B_W + c0 + width]
        return z * _sigmoid(z)

    for g, w in enumerate(POOL_WINDOWS):
        c0 = g * POOL_CH
        for rows in row_chunks(CONV_ROWS):
            cur = delayed(a_buf, 0, rows, c0, POOL_CH)
            win = cur
            for d in range(1, w):
                win = win + delayed(a_buf, d, rows, c0, POOL_CH)
            cnt = jnp.minimum(tpos[rows] + 1, w).astype(F32)
            pool_buf[rows, c0:c0 + POOL_CH] = (win / cnt - cur).astype(BF16)
        mix_buf[:, c0:c0 + POOL_CH] = _dot(pool_buf[:, c0:c0 + POOL_CH], pw_ref[g])
    for rows in row_chunks(EW_ROWS):
        y_buf[rows, 0:A_W] = (mix_buf[rows, :] * ps_ref[...] * silu_z(rows, 0, A_W)).astype(BF16)

    for rows in row_chunks(CONV_ROWS):
        for c0 in range(0, B_W, CONV_COLS):
            acc = jnp.broadcast_to(cb_ref[:, c0:c0 + CONV_COLS], (CONV_ROWS, CONV_COLS))
            for k in range(CONV_K):
                acc = acc + cw_ref[k:k + 1, c0:c0 + CONV_COLS] * delayed(u_buf, CONV_K - 1 - k, rows, c0, CONV_COLS)
            mix_buf[rows, c0:c0 + CONV_COLS] = acc
    for rows in row_chunks(EW_ROWS):
        conv = mix_buf[rows, :]
        mu = jnp.mean(conv, axis=-1, keepdims=True)
        cen = conv - mu
        var = jnp.mean(cen * cen, axis=-1, keepdims=True)
        yn = cen * lax.rsqrt(var + EPS) * cg_ref[...] + cbeta_ref[...]
        y_buf[rows, A_W:MIX_W] = (yn * _sigmoid(yn) * silu_z(rows, A_W, B_W)).astype(BF16)

    y_nat = _dot(from_perm, y_buf[...]).astype(BF16)
    o_ref[0] = x_ref[0] + _dot(y_nat, wout_ref[...])


def _pool_conv_layer(x, norm_g, w_in, pool_w, pool_scale, conv_w, conv_b, cn_g, cn_b, w_out):
    B, S, D = x.shape
    ts = SEQ_TILE
    const2 = lambda b, i: (0, 0)
    const3 = lambda b, i: (0, 0, 0)
    single = pl.Buffered(1)
    return pl.pallas_call(
        _pool_conv_kernel,
        grid=(B, S // ts),
        in_specs=[
            pl.BlockSpec((1, ts, D), lambda b, i: (b, i, 0)),
            pl.BlockSpec((1, D), const2),
            pl.BlockSpec((D, IN0_W), const2, pipeline_mode=single),
            pl.BlockSpec((len(POOL_WINDOWS), POOL_CH, POOL_CH), const3, pipeline_mode=single),
            pl.BlockSpec((1, A_W), const2),
            pl.BlockSpec((CONV_K, B_W), const2),
            pl.BlockSpec((1, B_W), const2),
            pl.BlockSpec((1, B_W), const2),
            pl.BlockSpec((1, B_W), const2),
            pl.BlockSpec((MIX_W, D), const2, pipeline_mode=single),
        ],
        out_specs=pl.BlockSpec((1, ts, D), lambda b, i: (b, i, 0)),
        out_shape=jax.ShapeDtypeStruct((B, S, D), F32),
        scratch_shapes=[
            pltpu.VMEM((2 * ts, A_W), F32),
            pltpu.VMEM((2 * ts, B_W), F32),
            pltpu.VMEM((ts, D), BF16),
            pltpu.VMEM((ts, D), BF16),
            pltpu.VMEM((ts, IN0_W - A_W), F32),
            pltpu.VMEM((ts, A_W), BF16),
            pltpu.VMEM((ts, A_W), F32),
            pltpu.VMEM((ts, MIX_W), BF16),
        ],
        compiler_params=pltpu.CompilerParams(
            dimension_semantics=("arbitrary", "arbitrary"), vmem_limit_bytes=VMEM_LIMIT),
        name="pool_conv_layer",
    )(x, norm_g.reshape(1, D), w_in.astype(BF16), pool_w.astype(BF16), pool_scale.reshape(1, A_W),
      conv_w, conv_b.reshape(1, B_W), cn_g.reshape(1, B_W), cn_b.reshape(1, B_W), w_out.astype(BF16))


def _head_rms_norm(t, g_row):
    low = lax.broadcasted_iota(jnp.int32, (1, LANES), 1) < HEAD_DIM
    cols = []
    for c in range(ATT_W // LANES):
        tc = t[:, c * LANES:(c + 1) * LANES]
        sq = tc * tc
        s_low = jnp.sum(jnp.where(low, sq, 0.0), axis=-1, keepdims=True)
        s_high = jnp.sum(jnp.where(low, 0.0, sq), axis=-1, keepdims=True)
        ms = jnp.where(low, s_low, s_high) * (1.0 / HEAD_DIM)
        cols.append(tc * lax.rsqrt(ms + EPS) * g_row[:, c * LANES:(c + 1) * LANES])
    return cols


def _moba_proj_kernel(x_ref, ng_ref, win_ref, qg_ref, kg_ref, q_ref, k_ref, vt_ref, gz_ref, km_ref):
    x = x_ref[0]
    h = _rms_norm_rows(x, ng_ref[...]).astype(BF16)

    q = _dot(h, win_ref[:, 0:ATT_W])
    for c, qc in enumerate(_head_rms_norm(q, qg_ref[...])):
        q_ref[0, :, c * LANES:(c + 1) * LANES] = (qc * (HEAD_DIM ** -0.5 * LOG2_E)).astype(BF16)

    k = _dot(h, win_ref[:, ATT_W:2 * ATT_W])
    for c, kc in enumerate(_head_rms_norm(k, kg_ref[...])):
        k_ref[0, :, c * LANES:(c + 1) * LANES] = kc.astype(BF16)
        km_ref[0, 0, :, c * LANES:(c + 1) * LANES] = jnp.mean(kc, axis=0, keepdims=True)

    v = _dot(h, win_ref[:, 2 * ATT_W:3 * ATT_W])
    vt_ref[0, 0] = v.T.astype(BF16)

    z = _dot(h, win_ref[:, 3 * ATT_W:4 * ATT_W])
    gz_ref[0] = (z * _sigmoid(z)).astype(BF16)


def _moba_proj(x, norm_g, w_in, q_norm_g, k_norm_g):
    B, S, D = x.shape
    ts = MOBA_BLOCK
    nb = S // ts
    const2 = lambda b, i: (0, 0)
    row_blk = pl.BlockSpec((1, ts, ATT_W), lambda b, i: (b, i, 0))
    return pl.pallas_call(
        _moba_proj_kernel,
        grid=(B, nb),
        in_specs=[
            pl.BlockSpec((1, ts, D), lambda b, i: (b, i, 0)),
            pl.BlockSpec((1, D), const2),
            pl.BlockSpec((D, 4 * ATT_W), const2, pipeline_mode=pl.Buffered(1)),
            pl.BlockSpec((1, ATT_W), const2),
            pl.BlockSpec((1, ATT_W), const2),
        ],
        out_specs=[
            row_blk,
            row_blk,
            pl.BlockSpec((1, 1, ATT_W, ts), lambda b, i: (b, i, 0, 0)),
            row_blk,
            pl.BlockSpec((1, 1, 1, ATT_W), lambda b, i: (b, i, 0, 0)),
        ],
        out_shape=[
            jax.ShapeDtypeStruct((B, S, ATT_W), BF16),
            jax.ShapeDtypeStruct((B, S, ATT_W), BF16),
            jax.ShapeDtypeStruct((B, nb, ATT_W, ts), BF16),
            jax.ShapeDtypeStruct((B, S, ATT_W), BF16),
            jax.ShapeDtypeStruct((B, nb, 1, ATT_W), F32),
        ],
        compiler_params=pltpu.CompilerParams(
            dimension_semantics=("arbitrary", "arbitrary"), vmem_limit_bytes=VMEM_LIMIT),
        name="moba_proj",
    )(x, norm_g.reshape(1, D), w_in.astype(BF16),
      jnp.tile(q_norm_g, N_HEADS).reshape(1, ATT_W), jnp.tile(k_norm_g, N_HEADS).reshape(1, ATT_W))


def _moba_attn_kernel(slopes_ref, q_lo_ref, q_hi_ref, k_ref, vt_ref, km_ref, gz_lo_ref, gz_hi_ref,
                      o_lo_ref, o_hi_ref, s_ref, p_ref, qa_ref, bias_ref):
    blk = MOBA_BLOCK
    nb = km_ref.shape[1]
    half = nb // 2
    p = pl.program_id(1)
    step = pl.program_id(2)
    neg_inf = jnp.float32(-jnp.inf)
    LO, HI = 0, 1
    q_blk = (step, nb - 1 - step)
    q_refs = (q_lo_ref, q_hi_ref)

    km = km_ref[0]
    row = lax.broadcasted_iota(jnp.int32, (LANES, 1), 0)
    lane = lax.broadcasted_iota(jnp.int32, (1, LANES), 1)
    blk_id = lax.broadcasted_iota(jnp.int32, (nb, blk), 0)
    key_off = lax.broadcasted_iota(jnp.int32, (blk, LANES), 0).astype(F32)
    k_extra = jnp.where(lane < 3, key_off, 0.0).astype(BF16)
    causal = (lax.broadcasted_iota(jnp.int32, (blk, blk), 1)
              >= lax.broadcasted_iota(jnp.int32, (blk, blk), 0))

    def fold8(t, op):
        parts = [t[r:r + SUBLANES] for r in range(0, t.shape[0], SUBLANES)]
        while len(parts) > 1:
            parts = [op(parts[a], parts[a + 1]) for a in range(0, len(parts), 2)]
        return parts[0]

    for hh in range(HEADS_PER_GROUP):
        cs = slopes_ref[p * HEADS_PER_GROUP + hh] * LOG2_E
        cs_v = jnp.full((LANES, blk), cs, F32)
        cs_hi = cs_v.astype(BF16).astype(F32)
        cs_mid = (cs_v - cs_hi).astype(BF16).astype(F32)
        cs_lo = cs_v - cs_hi - cs_mid
        q_extra = jnp.where(row == 0, cs_hi, jnp.where(row == 1, cs_mid, jnp.where(row == 2, cs_lo, 0.0)))
        q_extra = q_extra.astype(BF16)
        kmh = jnp.where(lane // HEAD_DIM == hh, km, 0.0)
        km_hi = kmh.astype(BF16)
        km_lo = (kmh - km_hi.astype(F32)).astype(BF16)
        for qb in (LO, HI):
            i = q_blk[qb]
            q_t = q_refs[qb][0].astype(F32).T
            q_head = jnp.where(row // HEAD_DIM == hh, q_t, 0.0).astype(BF16)
            qa_ref[qb, hh] = jnp.concatenate([q_head, q_extra], axis=0)

            gate = _dot(km_hi, q_head) + _dot(km_lo, q_head)
            rank = jnp.zeros((nb, blk), jnp.int32)
            for mth in range(nb):
                gm = gate[mth:mth + 1, :]
                beats = (gm > gate) | ((gm == gate) & (mth < blk_id))
                rank = rank + jnp.where(mth < i, jnp.where(beats, 1, 0), 0)
            keep = (rank < MOBA_TOPK) & (blk_id < i)
            dist = ((blk_id - i) * blk).astype(F32)
            bias_ref[qb, hh] = jnp.where(keep, cs * dist, jnp.where(blk_id == i, 0.0, neg_inf))

    def key_rows(j):
        return pl.ds(pl.multiple_of(j * blk, blk), blk)

    n_slots = nb + 1
    slot_is_lo, slot_key, slot_causal = [], [], []
    for t in range(n_slots):
        if t <= half:
            slot_is_lo.append(None)
            slot_key.append(q_blk[HI] - half + t)
            slot_causal.append(t == half)
        else:
            u = t - (half + 1)
            is_lo = u >= q_blk[HI] - half
            slot_is_lo.append(is_lo)
            slot_key.append(jnp.where(is_lo, u - (q_blk[HI] - half), u))
            slot_causal.append(t == n_slots - 1)

    def pick(is_lo, lo, hi):
        return hi if is_lo is None else jnp.where(is_lo, lo, hi)

    def bias_row(t, hh):
        qb = HI if slot_is_lo[t] is None else jnp.where(slot_is_lo[t], LO, HI)
        return bias_ref[qb, hh, pl.ds(slot_key[t], 1), :]

    heads = range(HEADS_PER_GROUP)

    m8 = [[jnp.full((SUBLANES, blk), neg_inf, F32) for _ in heads] for _ in (LO, HI)]
    l8 = [[jnp.zeros((SUBLANES, blk), F32) for _ in heads] for _ in (LO, HI)]
    m_row = [[None] * HEADS_PER_GROUP for _ in (LO, HI)]
    acc = [[None] * HEADS_PER_GROUP for _ in (LO, HI)]
    n_fixed = half + 1

    def pass1(t, hh):
        is_lo = slot_is_lo[t]
        k_aug = jnp.concatenate([k_ref[0, key_rows(slot_key[t]), :], k_extra], axis=1)
        qa = qa_ref[HI, hh] if is_lo is None else qa_ref[jnp.where(is_lo, LO, HI), hh]
        s = _dot(k_aug, qa)
        if slot_causal[t]:
            s = jnp.where(causal, s, neg_inf)
        s_ref[hh, t] = s
        top = fold8(s, jnp.maximum) + bias_row(t, hh)
        if is_lo is None:
            m8[HI][hh] = jnp.maximum(m8[HI][hh], top)
        else:
            m8[LO][hh] = jnp.maximum(m8[LO][hh], jnp.where(is_lo, top, neg_inf))
            m8[HI][hh] = jnp.maximum(m8[HI][hh], jnp.where(is_lo, neg_inf, top))

    def row_max(hh):
        for qb in (LO, HI):
            m_row[qb][hh] = jnp.max(m8[qb][hh], axis=0, keepdims=True)

    def pass2(t, hh):
        is_lo = slot_is_lo[t]
        shift = pick(is_lo, m_row[LO][hh], m_row[HI][hh]) - bias_row(t, hh)
        pr = jnp.exp2(s_ref[hh, t] - shift)
        p_ref[hh, t] = pr.astype(BF16)
        part = fold8(pr, jnp.add)
        if is_lo is None:
            l8[HI][hh] = l8[HI][hh] + part
        else:
            l8[LO][hh] = l8[LO][hh] + jnp.where(is_lo, part, 0.0)
            l8[HI][hh] = l8[HI][hh] + jnp.where(is_lo, 0.0, part)

    def value_product(hh):
        v_of = lambda t: vt_ref[0, slot_key[t], hh * HEAD_DIM:(hh + 1) * HEAD_DIM, :]
        v_fixed = jnp.concatenate([v_of(t) for t in range(n_fixed)], axis=1)
        pieces = []
        for t in range(n_fixed, n_slots):
            v32 = v_of(t).astype(F32)
            pieces.append(jnp.concatenate([jnp.where(slot_is_lo[t], v32, 0.0),
                                           jnp.where(slot_is_lo[t], 0.0, v32)], axis=0).astype(BF16))
        v_split = jnp.concatenate(pieces, axis=1)
        pv_fixed = _dot(v_fixed, p_ref[hh, 0:n_fixed].reshape(n_fixed * blk, blk))
        pv_split = _dot(v_split, p_ref[hh, n_fixed:n_slots].reshape((n_slots - n_fixed) * blk, blk))
        acc[LO][hh] = pv_split[0:HEAD_DIM]
        acc[HI][hh] = pv_fixed + pv_split[HEAD_DIM:2 * HEAD_DIM]

    for t in range(n_slots):
        pass1(t, 0)
    row_max(0)
    for t in range(n_slots):
        pass2(t, 0)
        pass1(t, 1)
    row_max(1)
    value_product(0)
    for t in range(n_slots):
        pass2(t, 1)
    value_product(1)

    for qb, o_ref, gz_ref in ((LO, o_lo_ref, gz_lo_ref), (HI, o_hi_ref, gz_hi_ref)):
        o_t = jnp.concatenate(
            [acc[qb][hh] / jnp.sum(l8[qb][hh], axis=0, keepdims=True) for hh in heads], axis=0)
        o_ref[0] = (o_t.T * gz_ref[0].astype(F32)).astype(BF16)


def _moba_attn(q, k, vt, kmean, gz, slopes):
    B, S, _ = q.shape
    blk = MOBA_BLOCK
    nb = S // blk
    n_groups = ATT_W // LANES
    half = nb // 2
    lo_blk = pl.BlockSpec((1, blk, LANES), lambda b, p, s: (b, s, p))
    hi_blk = pl.BlockSpec((1, blk, LANES), lambda b, p, s: (b, nb - 1 - s, p))
    o_lo, o_hi = pl.pallas_call(
        _moba_attn_kernel,
        grid=(B, n_groups, half),
        in_specs=[
            pl.BlockSpec(memory_space=pltpu.SMEM),
            lo_blk,
            hi_blk,
            pl.BlockSpec((1, S, LANES), lambda b, p, s: (b, 0, p)),
            pl.BlockSpec((1, nb, LANES, blk), lambda b, p, s: (b, 0, p, 0)),
            pl.BlockSpec((1, nb, LANES), lambda b, p, s: (b, 0, p)),
            lo_blk,
            hi_blk,
        ],
        out_specs=[
            lo_blk,
            pl.BlockSpec((1, blk, LANES), lambda b, p, s: (b, half - 1 - s, p)),
        ],
        out_shape=[jax.ShapeDtypeStruct((B, S // 2, ATT_W), BF16)] * 2,
        scratch_shapes=[
            pltpu.VMEM((HEADS_PER_GROUP, nb + 1, blk, blk), F32),
            pltpu.VMEM((HEADS_PER_GROUP, nb + 1, blk, blk), BF16),
            pltpu.VMEM((2, HEADS_PER_GROUP, 2 * LANES, blk), BF16),
            pltpu.VMEM((2, HEADS_PER_GROUP, nb, blk), F32),
        ],
        compiler_params=pltpu.CompilerParams(
            dimension_semantics=("arbitrary", "arbitrary", "arbitrary"), vmem_limit_bytes=VMEM_LIMIT),
        name="moba_attn",
    )(slopes, q, q, k, vt, kmean, gz, gz)
    return jnp.concatenate([o_lo, o_hi], axis=1)


def _moba_out_kernel(x_ref, og_ref, w_ref, o_ref):
    o_ref[0] = x_ref[0] + _dot(og_ref[0], w_ref[...])


def _moba_out(x, og, w_out):
    B, S, D = x.shape
    ts = 2 * SEQ_TILE
    return pl.pallas_call(
        _moba_out_kernel,
        grid=(B, S // ts),
        in_specs=[
            pl.BlockSpec((1, ts, D), lambda b, i: (b, i, 0)),
            pl.BlockSpec((1, ts, ATT_W), lambda b, i: (b, i, 0)),
            pl.BlockSpec((ATT_W, D), lambda b, i: (0, 0), pipeline_mode=pl.Buffered(1)),
        ],
        out_specs=pl.BlockSpec((1, ts, D), lambda b, i: (b, i, 0)),
        out_shape=jax.ShapeDtypeStruct((B, S, D), F32),
        compiler_params=pltpu.CompilerParams(
            dimension_semantics=("arbitrary", "arbitrary"), vmem_limit_bytes=VMEM_LIMIT),
        name="moba_out",
    )(x, og, w_out.astype(BF16))


def kernel(x, norm0_g, w_in0, pool_w, pool_scale, conv_w, conv_b, cnorm_g, cnorm_b, w_out0,
           norm1_g, w_in1, q_norm_g, k_norm_g, w_out1):
    assert x.shape[1] % MOBA_BLOCK == 0 and x.shape[2] == D_MODEL
    x = _pool_conv_layer(x, norm0_g[0], w_in0[0], pool_w[0], pool_scale[0], conv_w[0], conv_b[0],
                         cnorm_g[0], cnorm_b[0], w_out0[0])
    q, k, vt, gz, kmean = _moba_proj(x, norm1_g[0], w_in1[0], q_norm_g[0], k_norm_g[0])
    slopes = jnp.exp2(-8.0 * jnp.arange(1, N_HEADS + 1, dtype=F32) / N_HEADS)
    og = _moba_attn(q, k, vt, kmean.reshape(kmean.shape[0], -1, ATT_W), gz, slopes)
    return _moba_out(x, og, w_out1[0])
```

```python
import functools
import math

import jax
import jax.numpy as jnp
from jax import lax
from jax.experimental import pallas as pl
from jax.experimental.pallas import tpu as pltpu

D_MODEL = 1024
A_W = 1024
B_W = 1024
MIX_W = 2048
POOL_WINDOWS = (2, 4, 8, 16)
POOL_CH = 256
CONV_K = 31
IN0_W = A_W + 2 * B_W + MIX_W
HEAD_DIM = 64
N_HEADS = 16
ATT_W = 1024
MOBA_BLOCK = 256
MOBA_TOPK = 3
EPS = 1e-6
LOG2_E = math.log2(math.e)

SEQ_TILE = 256
EW_ROWS = 32
CONV_ROWS = 64
CONV_COLS = 256
LANES = 128
SUBLANES = 8
BF16_ROWS = 16
HEADS_PER_GROUP = LANES // HEAD_DIM
VMEM_LIMIT = 56 * 1024 * 1024

F32 = jnp.float32
BF16 = jnp.bfloat16


def _dot(a, b):
    return jnp.dot(a, b, preferred_element_type=F32)


def _rms_norm_rows(x, g):
    ms = jnp.mean(x * x, axis=-1, keepdims=True)
    return x * lax.rsqrt(ms + EPS) * g


def _sigmoid(x):
    return 1.0 / (1.0 + jnp.exp(-x))


def _pool_conv_kernel(x_ref, ng_ref, win_ref, pw_ref, ps_ref, cw_ref, cb_ref, cg_ref, cbeta_ref,
                      wout_ref, o_ref, a_buf, u_buf, h_buf, hp_buf, proj_buf, pool_buf, mix_buf, y_buf):
    ts = SEQ_TILE
    strand = ts // SUBLANES
    i = pl.program_id(1)

    r0 = lax.broadcasted_iota(jnp.int32, (ts, ts), 0)
    r1 = lax.broadcasted_iota(jnp.int32, (ts, ts), 1)
    to_perm = jnp.where(r1 == (r0 % SUBLANES) * strand + r0 // SUBLANES, 1.0, 0.0).astype(BF16)
    from_perm = jnp.where(r0 == (r1 % SUBLANES) * strand + r1 // SUBLANES, 1.0, 0.0).astype(BF16)
    row = lax.broadcasted_iota(jnp.int32, (ts, 1), 0)
    first_strand = row % SUBLANES == 0
    tpos = i * ts + (row % SUBLANES) * strand + row // SUBLANES

    @pl.when(i == 0)
    def _():
        a_buf[ts:2 * ts, :] = jnp.zeros((ts, A_W), F32)
        u_buf[ts:2 * ts, :] = jnp.zeros((ts, B_W), F32)

    def last_strand_to_front(buf):
        buf[0:ts, :] = pltpu.roll(buf[ts:2 * ts, :], ts - (SUBLANES - 1), 0)

    def finish_shift(buf):
        buf[0:ts, :] = jnp.where(first_strand, buf[0:ts, :], pltpu.roll(buf[ts:2 * ts, :], 1, 0))

    def delayed(buf, d, rows, c0, width):
        start = (strand - d) * SUBLANES + rows.start
        return buf[start:start + (rows.stop - rows.start), c0:c0 + width]

    def row_chunks(n):
        return [slice(r, r + n) for r in range(0, ts, n)]

    for rows in row_chunks(EW_ROWS):
        h_buf[rows, :] = _rms_norm_rows(x_ref[0, rows, :], ng_ref[...]).astype(BF16)
    hp_buf[...] = _dot(to_perm, h_buf[...]).astype(BF16)

    last_strand_to_front(a_buf)
    a_buf[ts:2 * ts, :] = _dot(hp_buf[...], win_ref[:, 0:A_W])
    finish_shift(a_buf)
    proj_buf[...] = _dot(hp_buf[...], win_ref[:, A_W:IN0_W])

    last_strand_to_front(u_buf)
    for rows in row_chunks(EW_ROWS):
        u_buf[ts + rows.start:ts + rows.stop, :] = proj_buf[rows, 0:B_W] * _sigmoid(proj_buf[rows, B_W:2 * B_W])
    finish_shift(u_buf)

    def silu_z(rows, c0, width):
        z = proj_buf[rows, 2 * B_W + c0:2 * B_W + c0 + width]
        return z * _sigmoid(z)

    for g, w in enumerate(POOL_WINDOWS):
        c0 = g * POOL_CH
        for rows in row_chunks(CONV_ROWS):
            cur = delayed(a_buf, 0, rows, c0, POOL_CH)
            win = cur
            for d in range(1, w):
                win = win + delayed(a_buf, d, rows, c0, POOL_CH)
            cnt = jnp.minimum(tpos[rows] + 1, w).astype(F32)
            pool_buf[rows, c0:c0 + POOL_CH] = (win / cnt - cur).astype(BF16)
        mix_buf[:, c0:c0 + POOL_CH] = _dot(pool_buf[:, c0:c0 + POOL_CH], pw_ref[g])
    for rows in row_chunks(EW_ROWS):
        y_buf[rows, 0:A_W] = (mix_buf[rows, :] * ps_ref[...] * silu_z(rows, 0, A_W)).astype(BF16)

    for rows in row_chunks(CONV_ROWS):
        for c0 in range(0, B_W, CONV_COLS):
            acc = jnp.broadcast_to(cb_ref[:, c0:c0 + CONV_COLS], (CONV_ROWS, CONV_COLS))
            for k in range(CONV_K):
                acc = acc + cw_ref[k:k + 1, c0:c0 + CONV_COLS] * delayed(u_buf, CONV_K - 1 - k, rows, c0, CONV_COLS)
            mix_buf[rows, c0:c0 + CONV_COLS] = acc
    for rows in row_chunks(EW_ROWS):
        conv = mix_buf[rows, :]
        mu = jnp.mean(conv, axis=-1, keepdims=True)
        cen = conv - mu
        var = jnp.mean(cen * cen, axis=-1, keepdims=True)
        yn = cen * lax.rsqrt(var + EPS) * cg_ref[...] + cbeta_ref[...]
        y_buf[rows, A_W:MIX_W] = (yn * _sigmoid(yn) * silu_z(rows, A_W, B_W)).astype(BF16)

    y_nat = _dot(from_perm, y_buf[...]).astype(BF16)
    o_ref[0] = x_ref[0] + _dot(y_nat, wout_ref[...])


def _pool_conv_layer(x, norm_g, w_in, pool_w, pool_scale, conv_w, conv_b, cn_g, cn_b, w_out):
    B, S, D = x.shape
    ts = SEQ_TILE
    const2 = lambda b, i: (0, 0)
    const3 = lambda b, i: (0, 0, 0)
    single = pl.Buffered(1)
    return pl.pallas_call(
        _pool_conv_kernel,
        grid=(B, S // ts),
        in_specs=[
            pl.BlockSpec((1, ts, D), lambda b, i: (b, i, 0)),
            pl.BlockSpec((1, D), const2),
            pl.BlockSpec((D, IN0_W), const2, pipeline_mode=single),
            pl.BlockSpec((len(POOL_WINDOWS), POOL_CH, POOL_CH), const3, pipeline_mode=single),
            pl.BlockSpec((1, A_W), const2),
            pl.BlockSpec((CONV_K, B_W), const2),
            pl.BlockSpec((1, B_W), const2),
            pl.BlockSpec((1, B_W), const2),
            pl.BlockSpec((1, B_W), const2),
            pl.BlockSpec((MIX_W, D), const2, pipeline_mode=single),
        ],
        out_specs=pl.BlockSpec((1, ts, D), lambda b, i: (b, i, 0)),
        out_shape=jax.ShapeDtypeStruct((B, S, D), F32),
        scratch_shapes=[
            pltpu.VMEM((2 * ts, A_W), F32),
            pltpu.VMEM((2 * ts, B_W), F32),
            pltpu.VMEM((ts, D), BF16),
            pltpu.VMEM((ts, D), BF16),
            pltpu.VMEM((ts, IN0_W - A_W), F32),
            pltpu.VMEM((ts, A_W), BF16),
            pltpu.VMEM((ts, A_W), F32),
            pltpu.VMEM((ts, MIX_W), BF16),
        ],
        compiler_params=pltpu.CompilerParams(
            dimension_semantics=("arbitrary", "arbitrary"), vmem_limit_bytes=VMEM_LIMIT),
        name="pool_conv_layer",
    )(x, norm_g.reshape(1, D), w_in.astype(BF16), pool_w.astype(BF16), pool_scale.reshape(1, A_W),
      conv_w, conv_b.reshape(1, B_W), cn_g.reshape(1, B_W), cn_b.reshape(1, B_W), w_out.astype(BF16))


def _head_rms_norm(t, g_row):
    low = lax.broadcasted_iota(jnp.int32, (1, LANES), 1) < HEAD_DIM
    cols = []
    for c in range(ATT_W // LANES):
        tc = t[:, c * LANES:(c + 1) * LANES]
        sq = tc * tc
        s_low = jnp.sum(jnp.where(low, sq, 0.0), axis=-1, keepdims=True)
        s_high = jnp.sum(jnp.where(low, 0.0, sq), axis=-1, keepdims=True)
        ms = jnp.where(low, s_low, s_high) * (1.0 / HEAD_DIM)
        cols.append(tc * lax.rsqrt(ms + EPS) * g_row[:, c * LANES:(c + 1) * LANES])
    return cols


def _moba_proj_kernel(x_ref, ng_ref, win_ref, qg_ref, kg_ref, q_ref, k_ref, vt_ref, gz_ref, km_ref):
    x = x_ref[0]
    h = _rms_norm_rows(x, ng_ref[...]).astype(BF16)

    q = _dot(h, win_ref[:, 0:ATT_W])
    for c, qc in enumerate(_head_rms_norm(q, qg_ref[...])):
        q_ref[0, :, c * LANES:(c + 1) * LANES] = (qc * (HEAD_DIM ** -0.5 * LOG2_E)).astype(BF16)

    k = _dot(h, win_ref[:, ATT_W:2 * ATT_W])
    for c, kc in enumerate(_head_rms_norm(k, kg_ref[...])):
        k_ref[0, :, c * LANES:(c + 1) * LANES] = kc.astype(BF16)
        km_ref[0, 0, :, c * LANES:(c + 1) * LANES] = jnp.mean(kc, axis=0, keepdims=True)

    v = _dot(h, win_ref[:, 2 * ATT_W:3 * ATT_W])
    vt_ref[0, 0] = v.T.astype(BF16)

    z = _dot(h, win_ref[:, 3 * ATT_W:4 * ATT_W])
    gz_ref[0] = (z * _sigmoid(z)).astype(BF16)


def _moba_proj(x, norm_g, w_in, q_norm_g, k_norm_g):
    B, S, D = x.shape
    ts = MOBA_BLOCK
    nb = S // ts
    const2 = lambda b, i: (0, 0)
    row_blk = pl.BlockSpec((1, ts, ATT_W), lambda b, i: (b, i, 0))
    return pl.pallas_call(
        _moba_proj_kernel,
        grid=(B, nb),
        in_specs=[
            pl.BlockSpec((1, ts, D), lambda b, i: (b, i, 0)),
            pl.BlockSpec((1, D), const2),
            pl.BlockSpec((D, 4 * ATT_W), const2, pipeline_mode=pl.Buffered(1)),
            pl.BlockSpec((1, ATT_W), const2),
            pl.BlockSpec((1, ATT_W), const2),
        ],
        out_specs=[
            row_blk,
            row_blk,
            pl.BlockSpec((1, 1, ATT_W, ts), lambda b, i: (b, i, 0, 0)),
            row_blk,
            pl.BlockSpec((1, 1, 1, ATT_W), lambda b, i: (b, i, 0, 0)),
        ],
        out_shape=[
            jax.ShapeDtypeStruct((B, S, ATT_W), BF16),
            jax.ShapeDtypeStruct((B, S, ATT_W), BF16),
            jax.ShapeDtypeStruct((B, nb, ATT_W, ts), BF16),
            jax.ShapeDtypeStruct((B, S, ATT_W), BF16),
            jax.ShapeDtypeStruct((B, nb, 1, ATT_W), F32),
        ],
        compiler_params=pltpu.CompilerParams(
            dimension_semantics=("arbitrary", "arbitrary"), vmem_limit_bytes=VMEM_LIMIT),
        name="moba_proj",
    )(x, norm_g.reshape(1, D), w_in.astype(BF16),
      jnp.tile(q_norm_g, N_HEADS).reshape(1, ATT_W), jnp.tile(k_norm_g, N_HEADS).reshape(1, ATT_W))


def _moba_attn_kernel(slopes_ref, q_lo_ref, q_hi_ref, k_ref, vt_ref, km_ref, gz_lo_ref, gz_hi_ref,
                      o_lo_ref, o_hi_ref, s_ref, p_ref, qa_ref, bias_ref):
    blk = MOBA_BLOCK
    nb = km_ref.shape[1]
    half = nb // 2
    p = pl.program_id(1)
    step = pl.program_id(2)
    neg_inf = jnp.float32(-jnp.inf)
    LO, HI = 0, 1
    q_blk = (step, nb - 1 - step)
    q_refs = (q_lo_ref, q_hi_ref)

    km = km_ref[0]
    row = lax.broadcasted_iota(jnp.int32, (LANES, 1), 0)
    lane = lax.broadcasted_iota(jnp.int32, (1, LANES), 1)
    blk_id = lax.broadcasted_iota(jnp.int32, (nb, blk), 0)
    key_off = lax.broadcasted_iota(jnp.int32, (blk, LANES), 0).astype(F32)
    k_extra = jnp.where(lane < 3, key_off, 0.0).astype(BF16)
    causal = (lax.broadcasted_iota(jnp.int32, (blk, blk), 1)
              >= lax.broadcasted_iota(jnp.int32, (blk, blk), 0))

    def fold8(t, op):
        parts = [t[r:r + SUBLANES] for r in range(0, t.shape[0], SUBLANES)]
        while len(parts) > 1:
            parts = [op(parts[a], parts[a + 1]) for a in range(0, len(parts), 2)]
        return parts[0]

    for hh in range(HEADS_PER_GROUP):
        cs = slopes_ref[p * HEADS_PER_GROUP + hh] * LOG2_E
        cs_v = jnp.full((LANES, blk), cs, F32)
        cs_hi = cs_v.astype(BF16).astype(F32)
        cs_mid = (cs_v - cs_hi).astype(BF16).astype(F32)
        cs_lo = cs_v - cs_hi - cs_mid
        q_extra = jnp.where(row == 0, cs_hi, jnp.where(row == 1, cs_mid, jnp.where(row == 2, cs_lo, 0.0)))
        q_extra = q_extra.astype(BF16)
        kmh = jnp.where(lane // HEAD_DIM == hh, km, 0.0)
        km_hi = kmh.astype(BF16)
        km_lo = (kmh - km_hi.astype(F32)).astype(BF16)
        for qb in (LO, HI):
            i = q_blk[qb]
            q_t = q_refs[qb][0].astype(F32).T
            q_head = jnp.where(row // HEAD_DIM == hh, q_t, 0.0).astype(BF16)
            qa_ref[qb, hh] = jnp.concatenate([q_head, q_extra], axis=0)

            gate = _dot(km_hi, q_head) + _dot(km_lo, q_head)
            rank = jnp.zeros((nb, blk), jnp.int32)
            for mth in range(half if qb == LO else nb - 1):
                gm = gate[mth:mth + 1, :]
                beats = (gm > gate) | ((gm == gate) & (mth < blk_id))
                rank = rank + jnp.where(mth < i, jnp.where(beats, 1, 0), 0)
            keep = (rank < MOBA_TOPK) & (blk_id < i)
            dist = ((blk_id - i) * blk).astype(F32)
            bias_ref[qb, hh] = jnp.where(keep, cs * dist, jnp.where(blk_id == i, 0.0, neg_inf))

    def key_rows(j):
        return pl.ds(pl.multiple_of(j * blk, blk), blk)

    n_slots = nb + 1
    slot_is_lo, slot_key, slot_causal = [], [], []
    for t in range(n_slots):
        if t <= half:
            slot_is_lo.append(None)
            slot_key.append(q_blk[HI] - half + t)
            slot_causal.append(t == half)
        else:
            u = t - (half + 1)
            is_lo = u >= q_blk[HI] - half
            slot_is_lo.append(is_lo)
            slot_key.append(jnp.where(is_lo, u - (q_blk[HI] - half), u))
            slot_causal.append(t == n_slots - 1)

    def pick(is_lo, lo, hi):
        return hi if is_lo is None else jnp.where(is_lo, lo, hi)

    def bias_row(t, hh):
        qb = HI if slot_is_lo[t] is None else jnp.where(slot_is_lo[t], LO, HI)
        return bias_ref[qb, hh, pl.ds(slot_key[t], 1), :]

    heads = range(HEADS_PER_GROUP)

    m8 = [[jnp.full((SUBLANES, blk), neg_inf, F32) for _ in heads] for _ in (LO, HI)]
    m_row = [[None] * HEADS_PER_GROUP for _ in (LO, HI)]
    acc = [[None] * HEADS_PER_GROUP for _ in (LO, HI)]
    norm = [[None] * HEADS_PER_GROUP for _ in (LO, HI)]
    n_fixed = half + 1

    def pass1(t, hh):
        is_lo = slot_is_lo[t]
        k_aug = jnp.concatenate([k_ref[0, key_rows(slot_key[t]), :], k_extra], axis=1)
        qa = qa_ref[HI, hh] if is_lo is None else qa_ref[jnp.where(is_lo, LO, HI), hh]
        s = _dot(k_aug, qa)
        if slot_causal[t]:
            s = jnp.where(causal, s, neg_inf)
        s_ref[hh, t] = s
        top = fold8(s, jnp.maximum) + bias_row(t, hh)
        if is_lo is None:
            m8[HI][hh] = jnp.maximum(m8[HI][hh], top)
        else:
            m8[LO][hh] = jnp.maximum(m8[LO][hh], jnp.where(is_lo, top, neg_inf))
            m8[HI][hh] = jnp.maximum(m8[HI][hh], jnp.where(is_lo, neg_inf, top))

    def row_max(hh):
        for qb in (LO, HI):
            m_row[qb][hh] = jnp.max(m8[qb][hh], axis=0, keepdims=True)

    def pass2(t, hh):
        shift = pick(slot_is_lo[t], m_row[LO][hh], m_row[HI][hh]) - bias_row(t, hh)
        p_ref[hh, t] = jnp.exp2(s_ref[hh, t] - shift).astype(BF16)

    ones_rows = jnp.ones((BF16_ROWS, blk), F32)
    v_rows = HEAD_DIM + BF16_ROWS

    def value_product(hh):
        v_of = lambda t: jnp.concatenate(
            [vt_ref[0, slot_key[t], hh * HEAD_DIM:(hh + 1) * HEAD_DIM, :].astype(F32), ones_rows], axis=0)
        v_fixed = jnp.concatenate([v_of(t).astype(BF16) for t in range(n_fixed)], axis=1)
        pieces = []
        for t in range(n_fixed, n_slots):
            v32 = v_of(t)
            pieces.append(jnp.concatenate([jnp.where(slot_is_lo[t], v32, 0.0),
                                           jnp.where(slot_is_lo[t], 0.0, v32)], axis=0).astype(BF16))
        v_split = jnp.concatenate(pieces, axis=1)
        pv_fixed = _dot(v_fixed, p_ref[hh, 0:n_fixed].reshape(n_fixed * blk, blk))
        pv_split = _dot(v_split, p_ref[hh, n_fixed:n_slots].reshape((n_slots - n_fixed) * blk, blk))
        pv_hi = pv_fixed + pv_split[v_rows:2 * v_rows]
        acc[LO][hh] = pv_split[0:HEAD_DIM]
        norm[LO][hh] = pv_split[HEAD_DIM:HEAD_DIM + 1]
        acc[HI][hh] = pv_hi[0:HEAD_DIM]
        norm[HI][hh] = pv_hi[HEAD_DIM:HEAD_DIM + 1]

    for t in range(n_slots):
        pass1(t, 0)
    row_max(0)
    for t in range(n_slots):
        pass2(t, 0)
        pass1(t, 1)
    row_max(1)
    value_product(0)
    for t in range(n_slots):
        pass2(t, 1)
    value_product(1)

    for qb, o_ref, gz_ref in ((LO, o_lo_ref, gz_lo_ref), (HI, o_hi_ref, gz_hi_ref)):
        o_t = jnp.concatenate([acc[qb][hh] / norm[qb][hh] for hh in heads], axis=0)
        o_ref[0] = (o_t.T * gz_ref[0].astype(F32)).astype(BF16)


def _moba_attn(q, k, vt, kmean, gz, slopes):
    B, S, _ = q.shape
    blk = MOBA_BLOCK
    nb = S // blk
    n_groups = ATT_W // LANES
    half = nb // 2
    lo_blk = pl.BlockSpec((1, blk, LANES), lambda b, p, s: (b, s, p))
    hi_blk = pl.BlockSpec((1, blk, LANES), lambda b, p, s: (b, nb - 1 - s, p))
    return pl.pallas_call(
        _moba_attn_kernel,
        grid=(B, n_groups, half),
        in_specs=[
            pl.BlockSpec(memory_space=pltpu.SMEM),
            lo_blk,
            hi_blk,
            pl.BlockSpec((1, S, LANES), lambda b, p, s: (b, 0, p)),
            pl.BlockSpec((1, nb, LANES, blk), lambda b, p, s: (b, 0, p, 0)),
            pl.BlockSpec((1, nb, LANES), lambda b, p, s: (b, 0, p)),
            lo_blk,
            hi_blk,
        ],
        out_specs=[
            lo_blk,
            pl.BlockSpec((1, blk, LANES), lambda b, p, s: (b, half - 1 - s, p)),
        ],
        out_shape=[jax.ShapeDtypeStruct((B, S // 2, ATT_W), BF16)] * 2,
        scratch_shapes=[
            pltpu.VMEM((HEADS_PER_GROUP, nb + 1, blk, blk), F32),
            pltpu.VMEM((HEADS_PER_GROUP, nb + 1, blk, blk), BF16),
            pltpu.VMEM((2, HEADS_PER_GROUP, 2 * LANES, blk), BF16),
            pltpu.VMEM((2, HEADS_PER_GROUP, nb, blk), F32),
        ],
        compiler_params=pltpu.CompilerParams(
            dimension_semantics=("arbitrary", "arbitrary", "arbitrary"), vmem_limit_bytes=VMEM_LIMIT),
        name="moba_attn",
    )(slopes, q, q, k, vt, kmean, gz, gz)


def _moba_out_kernel(x_ref, og_lo_ref, og_hi_ref, w_ref, o_ref, *, tiles_per_half):
    in_lo = pl.program_id(1) < tiles_per_half

    @pl.when(in_lo)
    def _():
        o_ref[0] = x_ref[0] + _dot(og_lo_ref[0], w_ref[...])

    @pl.when(jnp.logical_not(in_lo))
    def _():
        o_ref[0] = x_ref[0] + _dot(og_hi_ref[0], w_ref[...])


def _moba_out(x, og_lo, og_hi, w_out):
    B, S, D = x.shape
    ts = 2 * SEQ_TILE
    per_half = S // 2 // ts
    return pl.pallas_call(
        functools.partial(_moba_out_kernel, tiles_per_half=per_half),
        grid=(B, S // ts),
        in_specs=[
            pl.BlockSpec((1, ts, D), lambda b, i: (b, i, 0)),
            pl.BlockSpec((1, ts, ATT_W), lambda b, i: (b, jnp.minimum(i, per_half - 1), 0)),
            pl.BlockSpec((1, ts, ATT_W), lambda b, i: (b, jnp.maximum(i - per_half, 0), 0)),
            pl.BlockSpec((ATT_W, D), lambda b, i: (0, 0), pipeline_mode=pl.Buffered(1)),
        ],
        out_specs=pl.BlockSpec((1, ts, D), lambda b, i: (b, i, 0)),
        out_shape=jax.ShapeDtypeStruct((B, S, D), F32),
        compiler_params=pltpu.CompilerParams(
            dimension_semantics=("arbitrary", "arbitrary"), vmem_limit_bytes=VMEM_LIMIT),
        name="moba_out",
    )(x, og_lo, og_hi, w_out.astype(BF16))


def kernel(x, norm0_g, w_in0, pool_w, pool_scale, conv_w, conv_b, cnorm_g, cnorm_b, w_out0,
           norm1_g, w_in1, q_norm_g, k_norm_g, w_out1):
    assert x.shape[1] % MOBA_BLOCK == 0 and x.shape[2] == D_MODEL
    x = _pool_conv_layer(x, norm0_g[0], w_in0[0], pool_w[0], pool_scale[0], conv_w[0], conv_b[0],
                         cnorm_g[0], cnorm_b[0], w_out0[0])
    q, k, vt, gz, kmean = _moba_proj(x, norm1_g[0], w_in1[0], q_norm_g[0], k_norm_g[0])
    slopes = jnp.exp2(-8.0 * jnp.arange(1, N_HEADS + 1, dtype=F32) / N_HEADS)
    og_lo, og_hi = _moba_attn(q, k, vt, kmean.reshape(kmean.shape[0], -1, ATT_W), gz, slopes)
    return _moba_out(x, og_lo, og_hi, w_out1[0])
```

```python
import functools
import math

import jax
import jax.numpy as jnp
from jax import lax
from jax.experimental import pallas as pl
from jax.experimental.pallas import tpu as pltpu

D_MODEL = 1024
A_W = 1024
B_W = 1024
MIX_W = 2048
POOL_WINDOWS = (2, 4, 8, 16)
POOL_CH = 256
CONV_K = 31
IN0_W = A_W + 2 * B_W + MIX_W
HEAD_DIM = 64
N_HEADS = 16
ATT_W = 1024
MOBA_BLOCK = 256
MOBA_TOPK = 3
EPS = 1e-6
LOG2_E = math.log2(math.e)

SEQ_TILE = 256
EW_ROWS = 32
CONV_ROWS = 64
CONV_COLS = 256
LANES = 128
SUBLANES = 8
BF16_ROWS = 16
HEADS_PER_GROUP = LANES // HEAD_DIM
ATTN_HEADS_PER_STEP = 4
VMEM_LIMIT = 56 * 1024 * 1024

F32 = jnp.float32
BF16 = jnp.bfloat16


def _dot(a, b):
    return jnp.dot(a, b, preferred_element_type=F32)


def _rms_norm_rows(x, g):
    ms = jnp.mean(x * x, axis=-1, keepdims=True)
    return x * lax.rsqrt(ms + EPS) * g


def _sigmoid(x):
    return 1.0 / (1.0 + jnp.exp(-x))


def _pool_conv_kernel(x_ref, ng_ref, win_ref, pw_ref, ps_ref, cw_ref, cb_ref, cg_ref, cbeta_ref,
                      wout_ref, o_ref, a_buf, u_buf, h_buf, hp_buf, proj_buf, pool_buf, mix_buf, y_buf):
    ts = SEQ_TILE
    strand = ts // SUBLANES
    i = pl.program_id(1)

    r0 = lax.broadcasted_iota(jnp.int32, (ts, ts), 0)
    r1 = lax.broadcasted_iota(jnp.int32, (ts, ts), 1)
    to_perm = jnp.where(r1 == (r0 % SUBLANES) * strand + r0 // SUBLANES, 1.0, 0.0).astype(BF16)
    from_perm = jnp.where(r0 == (r1 % SUBLANES) * strand + r1 // SUBLANES, 1.0, 0.0).astype(BF16)
    row = lax.broadcasted_iota(jnp.int32, (ts, 1), 0)
    first_strand = row % SUBLANES == 0
    tpos = i * ts + (row % SUBLANES) * strand + row // SUBLANES

    @pl.when(i == 0)
    def _():
        a_buf[ts:2 * ts, :] = jnp.zeros((ts, A_W), F32)
        u_buf[ts:2 * ts, :] = jnp.zeros((ts, B_W), F32)

    def last_strand_to_front(buf):
        buf[0:ts, :] = pltpu.roll(buf[ts:2 * ts, :], ts - (SUBLANES - 1), 0)

    def finish_shift(buf):
        buf[0:ts, :] = jnp.where(first_strand, buf[0:ts, :], pltpu.roll(buf[ts:2 * ts, :], 1, 0))

    def delayed(buf, d, rows, c0, width):
        start = (strand - d) * SUBLANES + rows.start
        return buf[start:start + (rows.stop - rows.start), c0:c0 + width]

    def row_chunks(n):
        return [slice(r, r + n) for r in range(0, ts, n)]

    for rows in row_chunks(EW_ROWS):
        h_buf[rows, :] = _rms_norm_rows(x_ref[0, rows, :], ng_ref[...]).astype(BF16)
    hp_buf[...] = _dot(to_perm, h_buf[...]).astype(BF16)

    last_strand_to_front(a_buf)
    a_buf[ts:2 * ts, :] = _dot(hp_buf[...], win_ref[:, 0:A_W])
    finish_shift(a_buf)
    proj_buf[...] = _dot(hp_buf[...], win_ref[:, A_W:IN0_W])

    last_strand_to_front(u_buf)
    for rows in row_chunks(EW_ROWS):
        u_buf[ts + rows.start:ts + rows.stop, :] = proj_buf[rows, 0:B_W] * _sigmoid(proj_buf[rows, B_W:2 * B_W])
    finish_shift(u_buf)

    def silu_z(rows, c0, width):
        z = proj_buf[rows, 2 * B_W + c0:2 * B_W + c0 + width]
        return z * _sigmoid(z)

    for g, w in enumerate(POOL_WINDOWS):
        c0 = g * POOL_CH
        for rows in row_chunks(CONV_ROWS):
            cur = delayed(a_buf, 0, rows, c0, POOL_CH)
            win = cur
            for d in range(1, w):
                win = win + delayed(a_buf, d, rows, c0, POOL_CH)
            cnt = jnp.minimum(tpos[rows] + 1, w).astype(F32)
            pool_buf[rows, c0:c0 + POOL_CH] = (win / cnt - cur).astype(BF16)
        mix_buf[:, c0:c0 + POOL_CH] = _dot(pool_buf[:, c0:c0 + POOL_CH], pw_ref[g])
    for rows in row_chunks(EW_ROWS):
        y_buf[rows, 0:A_W] = (mix_buf[rows, :] * ps_ref[...] * silu_z(rows, 0, A_W)).astype(BF16)

    for rows in row_chunks(CONV_ROWS):
        for c0 in range(0, B_W, CONV_COLS):
            acc = jnp.broadcast_to(cb_ref[:, c0:c0 + CONV_COLS], (CONV_ROWS, CONV_COLS))
            for k in range(CONV_K):
                acc = acc + cw_ref[k:k + 1, c0:c0 + CONV_COLS] * delayed(u_buf, CONV_K - 1 - k, rows, c0, CONV_COLS)
            mix_buf[rows, c0:c0 + CONV_COLS] = acc
    for rows in row_chunks(EW_ROWS):
        conv = mix_buf[rows, :]
        mu = jnp.mean(conv, axis=-1, keepdims=True)
        cen = conv - mu
        var = jnp.mean(cen * cen, axis=-1, keepdims=True)
        yn = cen * lax.rsqrt(var + EPS) * cg_ref[...] + cbeta_ref[...]
        y_buf[rows, A_W:MIX_W] = (yn * _sigmoid(yn) * silu_z(rows, A_W, B_W)).astype(BF16)

    y_nat = _dot(from_perm, y_buf[...]).astype(BF16)
    o_ref[0] = x_ref[0] + _dot(y_nat, wout_ref[...])


def _pool_conv_layer(x, norm_g, w_in, pool_w, pool_scale, conv_w, conv_b, cn_g, cn_b, w_out):
    B, S, D = x.shape
    ts = SEQ_TILE
    const2 = lambda b, i: (0, 0)
    const3 = lambda b, i: (0, 0, 0)
    single = pl.Buffered(1)
    return pl.pallas_call(
        _pool_conv_kernel,
        grid=(B, S // ts),
        in_specs=[
            pl.BlockSpec((1, ts, D), lambda b, i: (b, i, 0)),
            pl.BlockSpec((1, D), const2),
            pl.BlockSpec((D, IN0_W), const2, pipeline_mode=single),
            pl.BlockSpec((len(POOL_WINDOWS), POOL_CH, POOL_CH), const3, pipeline_mode=single),
            pl.BlockSpec((1, A_W), const2),
            pl.BlockSpec((CONV_K, B_W), const2),
            pl.BlockSpec((1, B_W), const2),
            pl.BlockSpec((1, B_W), const2),
            pl.BlockSpec((1, B_W), const2),
            pl.BlockSpec((MIX_W, D), const2, pipeline_mode=single),
        ],
        out_specs=pl.BlockSpec((1, ts, D), lambda b, i: (b, i, 0)),
        out_shape=jax.ShapeDtypeStruct((B, S, D), F32),
        scratch_shapes=[
            pltpu.VMEM((2 * ts, A_W), F32),
            pltpu.VMEM((2 * ts, B_W), F32),
            pltpu.VMEM((ts, D), BF16),
            pltpu.VMEM((ts, D), BF16),
            pltpu.VMEM((ts, IN0_W - A_W), F32),
            pltpu.VMEM((ts, A_W), BF16),
            pltpu.VMEM((ts, A_W), F32),
            pltpu.VMEM((ts, MIX_W), BF16),
        ],
        compiler_params=pltpu.CompilerParams(
            dimension_semantics=("arbitrary", "arbitrary"), vmem_limit_bytes=VMEM_LIMIT),
        name="pool_conv_layer",
    )(x, norm_g.reshape(1, D), w_in.astype(BF16), pool_w.astype(BF16), pool_scale.reshape(1, A_W),
      conv_w, conv_b.reshape(1, B_W), cn_g.reshape(1, B_W), cn_b.reshape(1, B_W), w_out.astype(BF16))


def _head_rms_norm(t, g_row):
    low = lax.broadcasted_iota(jnp.int32, (1, LANES), 1) < HEAD_DIM
    cols = []
    for c in range(ATT_W // LANES):
        tc = t[:, c * LANES:(c + 1) * LANES]
        sq = tc * tc
        s_low = jnp.sum(jnp.where(low, sq, 0.0), axis=-1, keepdims=True)
        s_high = jnp.sum(jnp.where(low, 0.0, sq), axis=-1, keepdims=True)
        ms = jnp.where(low, s_low, s_high) * (1.0 / HEAD_DIM)
        cols.append(tc * lax.rsqrt(ms + EPS) * g_row[:, c * LANES:(c + 1) * LANES])
    return cols


def _moba_proj_kernel(x_ref, ng_ref, win_ref, qg_ref, kg_ref, q_ref, k_ref, vt_ref, gz_ref, km_ref):
    x = x_ref[0]
    h = _rms_norm_rows(x, ng_ref[...]).astype(BF16)

    q = _dot(h, win_ref[:, 0:ATT_W])
    for c, qc in enumerate(_head_rms_norm(q, qg_ref[...])):
        q_ref[0, :, c * LANES:(c + 1) * LANES] = (qc * (HEAD_DIM ** -0.5 * LOG2_E)).astype(BF16)

    k = _dot(h, win_ref[:, ATT_W:2 * ATT_W])
    for c, kc in enumerate(_head_rms_norm(k, kg_ref[...])):
        k_ref[0, :, c * LANES:(c + 1) * LANES] = kc.astype(BF16)
        km_ref[0, 0, :, c * LANES:(c + 1) * LANES] = jnp.mean(kc, axis=0, keepdims=True)

    v = _dot(h, win_ref[:, 2 * ATT_W:3 * ATT_W])
    vt_ref[0, 0] = v.T.astype(BF16)

    z = _dot(h, win_ref[:, 3 * ATT_W:4 * ATT_W])
    gz_ref[0] = (z * _sigmoid(z)).astype(BF16)


def _moba_proj(x, norm_g, w_in, q_norm_g, k_norm_g):
    B, S, D = x.shape
    ts = MOBA_BLOCK
    nb = S // ts
    const2 = lambda b, i: (0, 0)
    row_blk = pl.BlockSpec((1, ts, ATT_W), lambda b, i: (b, i, 0))
    return pl.pallas_call(
        _moba_proj_kernel,
        grid=(B, nb),
        in_specs=[
            pl.BlockSpec((1, ts, D), lambda b, i: (b, i, 0)),
            pl.BlockSpec((1, D), const2),
            pl.BlockSpec((D, 4 * ATT_W), const2, pipeline_mode=pl.Buffered(1)),
            pl.BlockSpec((1, ATT_W), const2),
            pl.BlockSpec((1, ATT_W), const2),
        ],
        out_specs=[
            row_blk,
            row_blk,
            pl.BlockSpec((1, 1, ATT_W, ts), lambda b, i: (b, i, 0, 0)),
            row_blk,
            pl.BlockSpec((1, 1, 1, ATT_W), lambda b, i: (b, i, 0, 0)),
        ],
        out_shape=[
            jax.ShapeDtypeStruct((B, S, ATT_W), BF16),
            jax.ShapeDtypeStruct((B, S, ATT_W), BF16),
            jax.ShapeDtypeStruct((B, nb, ATT_W, ts), BF16),
            jax.ShapeDtypeStruct((B, S, ATT_W), BF16),
            jax.ShapeDtypeStruct((B, nb, 1, ATT_W), F32),
        ],
        compiler_params=pltpu.CompilerParams(
            dimension_semantics=("arbitrary", "arbitrary"), vmem_limit_bytes=VMEM_LIMIT),
        name="moba_proj",
    )(x, norm_g.reshape(1, D), w_in.astype(BF16),
      jnp.tile(q_norm_g, N_HEADS).reshape(1, ATT_W), jnp.tile(k_norm_g, N_HEADS).reshape(1, ATT_W))


def _moba_attn_kernel(slopes_ref, q_lo_ref, q_hi_ref, k_ref, vt_ref, km_ref, gz_lo_ref, gz_hi_ref,
                      o_lo_ref, o_hi_ref, qa_ref, bias_ref, *head_scratch):
    blk = MOBA_BLOCK
    n_h = ATTN_HEADS_PER_STEP
    s_refs = head_scratch[:n_h]
    p_refs = head_scratch[n_h:]
    heads = range(n_h)

    def lanes_of(h):
        g0 = (h // HEADS_PER_GROUP) * LANES
        return slice(g0, g0 + LANES)
    nb = km_ref.shape[1]
    half = nb // 2
    p = pl.program_id(1)
    step = pl.program_id(2)
    neg_inf = jnp.float32(-jnp.inf)
    LO, HI = 0, 1
    q_blk = (step, nb - 1 - step)
    q_refs = (q_lo_ref, q_hi_ref)

    km = km_ref[0]
    row = lax.broadcasted_iota(jnp.int32, (LANES, 1), 0)
    lane = lax.broadcasted_iota(jnp.int32, (1, LANES), 1)
    blk_id = lax.broadcasted_iota(jnp.int32, (nb, blk), 0)
    key_off = lax.broadcasted_iota(jnp.int32, (blk, LANES), 0).astype(F32)
    k_extra = jnp.where(lane < 3, key_off, 0.0).astype(BF16)
    causal = (lax.broadcasted_iota(jnp.int32, (blk, blk), 1)
              >= lax.broadcasted_iota(jnp.int32, (blk, blk), 0))

    def fold8(t, op):
        parts = [t[r:r + SUBLANES] for r in range(0, t.shape[0], SUBLANES)]
        while len(parts) > 1:
            parts = [op(parts[a], parts[a + 1]) for a in range(0, len(parts), 2)]
        return parts[0]

    for h in heads:
        hh = h % HEADS_PER_GROUP
        cs = slopes_ref[p * n_h + h] * LOG2_E
        cs_v = jnp.full((LANES, blk), cs, F32)
        cs_hi = cs_v.astype(BF16).astype(F32)
        cs_mid = (cs_v - cs_hi).astype(BF16).astype(F32)
        cs_lo = cs_v - cs_hi - cs_mid
        q_extra = jnp.where(row == 0, cs_hi, jnp.where(row == 1, cs_mid, jnp.where(row == 2, cs_lo, 0.0)))
        q_extra = q_extra.astype(BF16)
        kmh = jnp.where(lane // HEAD_DIM == hh, km[:, lanes_of(h)], 0.0)
        km_hi = kmh.astype(BF16)
        km_lo = (kmh - km_hi.astype(F32)).astype(BF16)
        for qb in (LO, HI):
            i = q_blk[qb]
            q_t = q_refs[qb][0, :, lanes_of(h)].astype(F32).T
            q_head = jnp.where(row // HEAD_DIM == hh, q_t, 0.0).astype(BF16)
            qa_ref[qb, h] = jnp.concatenate([q_head, q_extra], axis=0)

            gate = _dot(km_hi, q_head) + _dot(km_lo, q_head)
            rank = jnp.zeros((nb, blk), jnp.int32)
            for mth in range(half if qb == LO else nb - 1):
                gm = gate[mth:mth + 1, :]
                beats = (gm > gate) | ((gm == gate) & (mth < blk_id))
                rank = rank + jnp.where(mth < i, jnp.where(beats, 1, 0), 0)
            keep = (rank < MOBA_TOPK) & (blk_id < i)
            dist = ((blk_id - i) * blk).astype(F32)
            bias_ref[qb, h] = jnp.where(keep, cs * dist, jnp.where(blk_id == i, 0.0, neg_inf))

    def key_rows(j):
        return pl.ds(pl.multiple_of(j * blk, blk), blk)

    n_slots = nb + 1
    slot_is_lo, slot_key, slot_causal = [], [], []
    for t in range(n_slots):
        if t <= half:
            slot_is_lo.append(None)
            slot_key.append(q_blk[HI] - half + t)
            slot_causal.append(t == half)
        else:
            u = t - (half + 1)
            is_lo = u >= q_blk[HI] - half
            slot_is_lo.append(is_lo)
            slot_key.append(jnp.where(is_lo, u - (q_blk[HI] - half), u))
            slot_causal.append(t == n_slots - 1)

    def pick(is_lo, lo, hi):
        return hi if is_lo is None else jnp.where(is_lo, lo, hi)

    def bias_row(t, hh):
        qb = HI if slot_is_lo[t] is None else jnp.where(slot_is_lo[t], LO, HI)
        return bias_ref[qb, hh, pl.ds(slot_key[t], 1), :]


    m8 = [[jnp.full((SUBLANES, blk), neg_inf, F32) for _ in heads] for _ in (LO, HI)]
    m_row = [[None] * n_h for _ in (LO, HI)]
    acc = [[None] * n_h for _ in (LO, HI)]
    norm = [[None] * n_h for _ in (LO, HI)]
    n_fixed = half + 1

    def pass1(t, hh):
        is_lo = slot_is_lo[t]
        k_aug = jnp.concatenate([k_ref[0, key_rows(slot_key[t]), lanes_of(hh)], k_extra], axis=1)
        qa = qa_ref[HI, hh] if is_lo is None else qa_ref[jnp.where(is_lo, LO, HI), hh]
        s = _dot(k_aug, qa)
        if slot_causal[t]:
            s = jnp.where(causal, s, neg_inf)
        s_refs[hh][t] = s
        top = fold8(s, jnp.maximum) + bias_row(t, hh)
        if is_lo is None:
            m8[HI][hh] = jnp.maximum(m8[HI][hh], top)
        else:
            m8[LO][hh] = jnp.maximum(m8[LO][hh], jnp.where(is_lo, top, neg_inf))
            m8[HI][hh] = jnp.maximum(m8[HI][hh], jnp.where(is_lo, neg_inf, top))

    def row_max(hh):
        for qb in (LO, HI):
            m_row[qb][hh] = jnp.max(m8[qb][hh], axis=0, keepdims=True)

    def pass2(t, hh):
        shift = pick(slot_is_lo[t], m_row[LO][hh], m_row[HI][hh]) - bias_row(t, hh)
        p_refs[hh][t] = jnp.exp2(s_refs[hh][t] - shift).astype(BF16)

    ones_rows = jnp.ones((BF16_ROWS, blk), F32)
    v_rows = HEAD_DIM + BF16_ROWS

    def values_and_ones(t, hh):
        return jnp.concatenate(
            [vt_ref[0, slot_key[t], hh * HEAD_DIM:(hh + 1) * HEAD_DIM, :].astype(F32), ones_rows], axis=0)

    pv_fixed = [None] * n_h

    def value_product_fixed(hh):
        v_fixed = jnp.concatenate([values_and_ones(t, hh).astype(BF16) for t in range(n_fixed)], axis=1)
        pv_fixed[hh] = _dot(v_fixed, p_refs[hh][0:n_fixed].reshape(n_fixed * blk, blk))

    def value_product_split(hh):
        pieces = []
        for t in range(n_fixed, n_slots):
            v32 = values_and_ones(t, hh)
            pieces.append(jnp.concatenate([jnp.where(slot_is_lo[t], v32, 0.0),
                                           jnp.where(slot_is_lo[t], 0.0, v32)], axis=0).astype(BF16))
        v_split = jnp.concatenate(pieces, axis=1)
        pv_split = _dot(v_split, p_refs[hh][n_fixed:n_slots].reshape((n_slots - n_fixed) * blk, blk))
        pv_hi = pv_fixed[hh] + pv_split[v_rows:2 * v_rows]
        acc[LO][hh] = pv_split[0:HEAD_DIM]
        norm[LO][hh] = pv_split[HEAD_DIM:HEAD_DIM + 1]
        acc[HI][hh] = pv_hi[0:HEAD_DIM]
        norm[HI][hh] = pv_hi[HEAD_DIM:HEAD_DIM + 1]

    for t in range(n_slots):
        pass1(t, 0)
    row_max(0)
    for h in heads:
        for t in range(n_slots):
            pass2(t, h)
            if h + 1 < n_h:
                pass1(t, h + 1)
            if t == 2 and h > 0:
                value_product_split(h - 1)
            if t == n_fixed:
                value_product_fixed(h)
        if h + 1 < n_h:
            row_max(h + 1)
    value_product_split(n_h - 1)

    for qb, o_ref, gz_ref in ((LO, o_lo_ref, gz_lo_ref), (HI, o_hi_ref, gz_hi_ref)):
        o_t = jnp.concatenate([acc[qb][hh] / norm[qb][hh] for hh in heads], axis=0)
        o_ref[0] = (o_t.T * gz_ref[0].astype(F32)).astype(BF16)


def _moba_attn(q, k, vt, kmean, gz, slopes):
    B, S, _ = q.shape
    blk = MOBA_BLOCK
    nb = S // blk
    n_h = ATTN_HEADS_PER_STEP
    width = n_h * HEAD_DIM
    half = nb // 2
    lo_blk = pl.BlockSpec((1, blk, width), lambda b, p, s: (b, s, p))
    hi_blk = pl.BlockSpec((1, blk, width), lambda b, p, s: (b, nb - 1 - s, p))
    return pl.pallas_call(
        _moba_attn_kernel,
        grid=(B, ATT_W // width, half),
        in_specs=[
            pl.BlockSpec(memory_space=pltpu.SMEM),
            lo_blk,
            hi_blk,
            pl.BlockSpec((1, S, width), lambda b, p, s: (b, 0, p)),
            pl.BlockSpec((1, nb, width, blk), lambda b, p, s: (b, 0, p, 0)),
            pl.BlockSpec((1, nb, width), lambda b, p, s: (b, 0, p)),
            lo_blk,
            hi_blk,
        ],
        out_specs=[
            lo_blk,
            pl.BlockSpec((1, blk, width), lambda b, p, s: (b, half - 1 - s, p)),
        ],
        out_shape=[jax.ShapeDtypeStruct((B, S // 2, ATT_W), BF16)] * 2,
        scratch_shapes=[
            pltpu.VMEM((2, n_h, 2 * LANES, blk), BF16),
            pltpu.VMEM((2, n_h, nb, blk), F32),
        ] + [pltpu.VMEM((nb + 1, blk, blk), F32)] * n_h
          + [pltpu.VMEM((nb + 1, blk, blk), BF16)] * n_h,
        compiler_params=pltpu.CompilerParams(
            dimension_semantics=("arbitrary", "arbitrary", "arbitrary"), vmem_limit_bytes=VMEM_LIMIT),
        name="moba_attn",
    )(slopes, q, q, k, vt, kmean, gz, gz)


def _moba_out_kernel(x_ref, og_lo_ref, og_hi_ref, w_ref, o_ref, *, tiles_per_half):
    in_lo = pl.program_id(1) < tiles_per_half

    @pl.when(in_lo)
    def _():
        o_ref[0] = x_ref[0] + _dot(og_lo_ref[0], w_ref[...])

    @pl.when(jnp.logical_not(in_lo))
    def _():
        o_ref[0] = x_ref[0] + _dot(og_hi_ref[0], w_ref[...])


def _moba_out(x, og_lo, og_hi, w_out):
    B, S, D = x.shape
    ts = 2 * SEQ_TILE
    per_half = S // 2 // ts
    return pl.pallas_call(
        functools.partial(_moba_out_kernel, tiles_per_half=per_half),
        grid=(B, S // ts),
        in_specs=[
            pl.BlockSpec((1, ts, D), lambda b, i: (b, i, 0)),
            pl.BlockSpec((1, ts, ATT_W), lambda b, i: (b, jnp.minimum(i, per_half - 1), 0)),
            pl.BlockSpec((1, ts, ATT_W), lambda b, i: (b, jnp.maximum(i - per_half, 0), 0)),
            pl.BlockSpec((ATT_W, D), lambda b, i: (0, 0), pipeline_mode=pl.Buffered(1)),
        ],
        out_specs=pl.BlockSpec((1, ts, D), lambda b, i: (b, i, 0)),
        out_shape=jax.ShapeDtypeStruct((B, S, D), F32),
        compiler_params=pltpu.CompilerParams(
            dimension_semantics=("arbitrary", "arbitrary"), vmem_limit_bytes=VMEM_LIMIT),
        name="moba_out",
    )(x, og_lo, og_hi, w_out.astype(BF16))


def kernel(x, norm0_g, w_in0, pool_w, pool_scale, conv_w, conv_b, cnorm_g, cnorm_b, w_out0,
           norm1_g, w_in1, q_norm_g, k_norm_g, w_out1):
    assert x.shape[1] % MOBA_BLOCK == 0 and x.shape[2] == D_MODEL
    x = _pool_conv_layer(x, norm0_g[0], w_in0[0], pool_w[0], pool_scale[0], conv_w[0], conv_b[0],
                         cnorm_g[0], cnorm_b[0], w_out0[0])
    q, k, vt, gz, kmean = _moba_proj(x, norm1_g[0], w_in1[0], q_norm_g[0], k_norm_g[0])
    slopes = jnp.exp2(-8.0 * jnp.arange(1, N_HEADS + 1, dtype=F32) / N_HEADS)
    og_lo, og_hi = _moba_attn(q, k, vt, kmean.reshape(kmean.shape[0], -1, ATT_W), gz, slopes)
    return _moba_out(x, og_lo, og_hi, w_out1[0])
```

```python
import functools
import math

import jax
import jax.numpy as jnp
from jax import lax
from jax.experimental import pallas as pl
from jax.experimental.pallas import tpu as pltpu

D_MODEL = 1024
A_W = 1024
B_W = 1024
MIX_W = 2048
POOL_WINDOWS = (2, 4, 8, 16)
POOL_CH = 256
CONV_K = 31
IN0_W = A_W + 2 * B_W + MIX_W
HEAD_DIM = 64
N_HEADS = 16
ATT_W = 1024
MOBA_BLOCK = 256
MOBA_TOPK = 3
EPS = 1e-6
LOG2_E = math.log2(math.e)

SEQ_TILE = 256
EW_ROWS = 32
CONV_ROWS = 64
CONV_COLS = 256
LANES = 128
SUBLANES = 8
BF16_ROWS = 16
HEADS_PER_GROUP = LANES // HEAD_DIM
ATTN_HEADS_PER_STEP = 8
ATTN_SCRATCH_SETS = 3
VMEM_LIMIT = 56 * 1024 * 1024

F32 = jnp.float32
BF16 = jnp.bfloat16


def _dot(a, b):
    return jnp.dot(a, b, preferred_element_type=F32)


def _rms_norm_rows(x, g):
    ms = jnp.mean(x * x, axis=-1, keepdims=True)
    return x * lax.rsqrt(ms + EPS) * g


def _sigmoid(x):
    return 1.0 / (1.0 + jnp.exp(-x))


def _pool_conv_kernel(x_ref, ng_ref, win_ref, pw_ref, ps_ref, cw_ref, cb_ref, cg_ref, cbeta_ref,
                      wout_ref, o_ref, a_buf, u_buf, h_buf, hp_buf, proj_buf, pool_buf, mix_buf, conv_buf,
                      y_buf, ynat_buf):
    ts = SEQ_TILE
    strand = ts // SUBLANES
    i = pl.program_id(1)

    r0 = lax.broadcasted_iota(jnp.int32, (ts, ts), 0)
    r1 = lax.broadcasted_iota(jnp.int32, (ts, ts), 1)
    to_perm = jnp.where(r1 == (r0 % SUBLANES) * strand + r0 // SUBLANES, 1.0, 0.0).astype(BF16)
    from_perm = jnp.where(r0 == (r1 % SUBLANES) * strand + r1 // SUBLANES, 1.0, 0.0).astype(BF16)
    row = lax.broadcasted_iota(jnp.int32, (ts, 1), 0)
    first_strand = row % SUBLANES == 0
    tpos = i * ts + (row % SUBLANES) * strand + row // SUBLANES

    @pl.when(i == 0)
    def _():
        a_buf[ts:2 * ts, :] = jnp.zeros((ts, A_W), F32)
        u_buf[ts:2 * ts, :] = jnp.zeros((ts, B_W), F32)

    def last_strand_to_front(buf, cols):
        buf[0:ts, cols] = pltpu.roll(buf[ts:2 * ts, cols], ts - (SUBLANES - 1), 0)

    def finish_shift(buf, cols):
        buf[0:ts, cols] = jnp.where(first_strand, buf[0:ts, cols], pltpu.roll(buf[ts:2 * ts, cols], 1, 0))

    def delayed(buf, d, rows, cols):
        start = (strand - d) * SUBLANES + rows.start
        return buf[start:start + (rows.stop - rows.start), cols]

    def row_chunks(n):
        return [slice(r, r + n) for r in range(0, ts, n)]

    def shifted_cols(cols, by):
        return slice(cols.start + by, cols.stop + by)

    def silu_z(rows, cols):
        z = proj_buf[rows, shifted_cols(cols, 2 * B_W)]
        return z * _sigmoid(z)

    col_chunks = [slice(c, c + CONV_COLS) for c in range(0, B_W, CONV_COLS)]
    conv_rows = row_chunks(CONV_ROWS)
    ew_rows = row_chunks(EW_ROWS)
    matmul_steps, vector_steps = [], []

    def step(queue, name, cost, needs, fn):
        queue.append((name, cost, tuple(needs), fn))

    def in_proj(c0):
        def run():
            proj_buf[:, c0:c0 + CONV_COLS] = _dot(hp_buf[...], win_ref[:, A_W + c0:A_W + c0 + CONV_COLS])
        return run

    n_z = MIX_W // CONV_COLS
    for c, cols in enumerate(col_chunks):
        step(matmul_steps, ("bv", c), 256, [], in_proj(cols.start))
        step(matmul_steps, ("bg", c), 256, [], in_proj(B_W + cols.start))
    for c, cols in enumerate(col_chunks):
        def pool_in(cols=cols):
            last_strand_to_front(a_buf, cols)
            a_buf[ts:2 * ts, cols] = _dot(hp_buf[...], win_ref[:, cols])
            finish_shift(a_buf, cols)
        step(matmul_steps, ("a", c), 256, [], pool_in)
    for j in range(n_z):
        step(matmul_steps, ("z", j), 256, [], in_proj(2 * B_W + j * CONV_COLS))
    for g in range(len(POOL_WINDOWS)):
        def pool_mix(g=g):
            gc = col_chunks[g]
            mix_buf[:, gc] = _dot(pool_buf[:, gc], pw_ref[g])
        step(matmul_steps, ("pool_mix", g), 64, [("pool", g, r) for r in range(len(conv_rows))], pool_mix)

    def unpermute(cols):
        def run():
            ynat_buf[:, cols] = _dot(from_perm, y_buf[:, cols]).astype(BF16)
        return run

    pool_cols, conv_cols = slice(0, A_W), slice(A_W, MIX_W)

    def out_first():
        o_ref[0] = x_ref[0] + _dot(ynat_buf[:, pool_cols], wout_ref[pool_cols, :])

    def out_second():
        o_ref[0] += _dot(ynat_buf[:, conv_cols], wout_ref[conv_cols, :])

    step(matmul_steps, "unperm_pool", 256, [("pool_gate", r) for r in range(len(ew_rows))], unpermute(pool_cols))
    step(matmul_steps, "out_pool", 1024, ["unperm_pool"], out_first)
    step(matmul_steps, "unperm_conv", 256, [("norm_gate", r) for r in range(len(ew_rows))], unpermute(conv_cols))
    step(matmul_steps, "out_conv", 1024, ["unperm_conv", "out_pool"], out_second)

    for c, cols in enumerate(col_chunks):
        def glu(cols=cols):
            last_strand_to_front(u_buf, cols)
            for rows in conv_rows:
                u_buf[ts + rows.start:ts + rows.stop, cols] = (
                    proj_buf[rows, cols] * _sigmoid(proj_buf[rows, shifted_cols(cols, B_W)]))
            finish_shift(u_buf, cols)
        step(vector_steps, ("glu", c), 200, [("bv", c), ("bg", c)], glu)
        for r, rows in enumerate(conv_rows):
            def conv(rows=rows, cols=cols):
                acc = jnp.broadcast_to(cb_ref[:, cols], (CONV_ROWS, CONV_COLS))
                for k in range(CONV_K):
                    acc = acc + cw_ref[k:k + 1, cols] * delayed(u_buf, CONV_K - 1 - k, rows, cols)
                conv_buf[rows, cols] = acc
            step(vector_steps, ("conv", c, r), 250, [("glu", c)], conv)

    for g, w in enumerate(POOL_WINDOWS):
        for r, rows in enumerate(conv_rows):
            def pool(g=g, w=w, rows=rows):
                cols = col_chunks[g]
                cur = delayed(a_buf, 0, rows, cols)
                win = cur
                for d in range(1, w):
                    win = win + delayed(a_buf, d, rows, cols)
                cnt = jnp.minimum(tpos[rows] + 1, w).astype(F32)
                pool_buf[rows, cols] = (win / cnt - cur).astype(BF16)
            step(vector_steps, ("pool", g, r), 30, [("a", g)], pool)
    for r, rows in enumerate(ew_rows):
        def pool_gate(rows=rows):
            y_buf[rows, pool_cols] = (mix_buf[rows, :] * ps_ref[...] * silu_z(rows, pool_cols)).astype(BF16)
        needs = [("pool_mix", g) for g in range(len(POOL_WINDOWS))] + [("z", j) for j in range(n_z // 2)]
        step(vector_steps, ("pool_gate", r), 80, needs, pool_gate)

    for r, rows in enumerate(ew_rows):
        def norm_gate(rows=rows):
            conv = conv_buf[rows, :]
            mu = jnp.mean(conv, axis=-1, keepdims=True)
            cen = conv - mu
            var = jnp.mean(cen * cen, axis=-1, keepdims=True)
            yn = cen * lax.rsqrt(var + EPS) * cg_ref[...] + cbeta_ref[...]
            y_buf[rows, conv_cols] = (yn * _sigmoid(yn) * silu_z(rows, shifted_cols(pool_cols, A_W))).astype(BF16)
        needs = ([("conv", c, q) for c in range(len(col_chunks)) for q in range(len(conv_rows))]
                 + [("z", j) for j in range(n_z // 2, n_z)])
        step(vector_steps, ("norm_gate", r), 190, needs, norm_gate)

    def emit(queues):
        done, spent, pos = set(), [0] * len(queues), [0] * len(queues)
        while any(pos[q] < len(queues[q]) for q in range(len(queues))):
            ready = [q for q in range(len(queues)) if pos[q] < len(queues[q])
                     and all(n in done for n in queues[q][pos[q]][2])]
            assert ready, "step ordering deadlock"
            q = min(ready, key=lambda q: spent[q])
            name, cost, _, fn = queues[q][pos[q]]
            fn()
            done.add(name)
            spent[q] += cost
            pos[q] += 1

    for rows in ew_rows:
        h_buf[rows, :] = _rms_norm_rows(x_ref[0, rows, :], ng_ref[...]).astype(BF16)
    hp_buf[...] = _dot(to_perm, h_buf[...]).astype(BF16)
    emit([matmul_steps, vector_steps])


def _pool_conv_layer(x, norm_g, w_in, pool_w, pool_scale, conv_w, conv_b, cn_g, cn_b, w_out):
    B, S, D = x.shape
    ts = SEQ_TILE
    const2 = lambda b, i: (0, 0)
    const3 = lambda b, i: (0, 0, 0)
    single = pl.Buffered(1)
    return pl.pallas_call(
        _pool_conv_kernel,
        grid=(B, S // ts),
        in_specs=[
            pl.BlockSpec((1, ts, D), lambda b, i: (b, i, 0)),
            pl.BlockSpec((1, D), const2),
            pl.BlockSpec((D, IN0_W), const2, pipeline_mode=single),
            pl.BlockSpec((len(POOL_WINDOWS), POOL_CH, POOL_CH), const3, pipeline_mode=single),
            pl.BlockSpec((1, A_W), const2),
            pl.BlockSpec((CONV_K, B_W), const2),
            pl.BlockSpec((1, B_W), const2),
            pl.BlockSpec((1, B_W), const2),
            pl.BlockSpec((1, B_W), const2),
            pl.BlockSpec((MIX_W, D), const2, pipeline_mode=single),
        ],
        out_specs=pl.BlockSpec((1, ts, D), lambda b, i: (b, i, 0)),
        out_shape=jax.ShapeDtypeStruct((B, S, D), F32),
        scratch_shapes=[
            pltpu.VMEM((2 * ts, A_W), F32),
            pltpu.VMEM((2 * ts, B_W), F32),
            pltpu.VMEM((ts, D), BF16),
            pltpu.VMEM((ts, D), BF16),
            pltpu.VMEM((ts, IN0_W - A_W), F32),
            pltpu.VMEM((ts, A_W), BF16),
            pltpu.VMEM((ts, A_W), F32),
            pltpu.VMEM((ts, B_W), F32),
            pltpu.VMEM((ts, MIX_W), BF16),
            pltpu.VMEM((ts, MIX_W), BF16),
        ],
        compiler_params=pltpu.CompilerParams(
            dimension_semantics=("arbitrary", "arbitrary"), vmem_limit_bytes=VMEM_LIMIT),
        name="pool_conv_layer",
    )(x, norm_g.reshape(1, D), w_in.astype(BF16), pool_w.astype(BF16), pool_scale.reshape(1, A_W),
      conv_w, conv_b.reshape(1, B_W), cn_g.reshape(1, B_W), cn_b.reshape(1, B_W), w_out.astype(BF16))


def _head_rms_norm(t, g_row):
    low = lax.broadcasted_iota(jnp.int32, (1, LANES), 1) < HEAD_DIM
    cols = []
    for c in range(ATT_W // LANES):
        tc = t[:, c * LANES:(c + 1) * LANES]
        sq = tc * tc
        s_low = jnp.sum(jnp.where(low, sq, 0.0), axis=-1, keepdims=True)
        s_high = jnp.sum(jnp.where(low, 0.0, sq), axis=-1, keepdims=True)
        ms = jnp.where(low, s_low, s_high) * (1.0 / HEAD_DIM)
        cols.append(tc * lax.rsqrt(ms + EPS) * g_row[:, c * LANES:(c + 1) * LANES])
    return cols


def _moba_proj_kernel(x_ref, ng_ref, win_ref, qg_ref, kg_ref, q_ref, k_ref, vt_ref, gz_ref, km_ref):
    x = x_ref[0]
    h = _rms_norm_rows(x, ng_ref[...]).astype(BF16)

    q = _dot(h, win_ref[:, 0:ATT_W])
    for c, qc in enumerate(_head_rms_norm(q, qg_ref[...])):
        q_ref[0, :, c * LANES:(c + 1) * LANES] = (qc * (HEAD_DIM ** -0.5 * LOG2_E)).astype(BF16)

    k = _dot(h, win_ref[:, ATT_W:2 * ATT_W])
    for c, kc in enumerate(_head_rms_norm(k, kg_ref[...])):
        k_ref[0, :, c * LANES:(c + 1) * LANES] = kc.astype(BF16)
        km_ref[0, 0, :, c * LANES:(c + 1) * LANES] = jnp.mean(kc, axis=0, keepdims=True)

    v = _dot(h, win_ref[:, 2 * ATT_W:3 * ATT_W])
    vt_ref[0, 0] = v.T.astype(BF16)

    z = _dot(h, win_ref[:, 3 * ATT_W:4 * ATT_W])
    gz_ref[0] = (z * _sigmoid(z)).astype(BF16)


def _moba_proj(x, norm_g, w_in, q_norm_g, k_norm_g):
    B, S, D = x.shape
    ts = MOBA_BLOCK
    nb = S // ts
    const2 = lambda b, i: (0, 0)
    row_blk = pl.BlockSpec((1, ts, ATT_W), lambda b, i: (b, i, 0))
    return pl.pallas_call(
        _moba_proj_kernel,
        grid=(B, nb),
        in_specs=[
            pl.BlockSpec((1, ts, D), lambda b, i: (b, i, 0)),
            pl.BlockSpec((1, D), const2),
            pl.BlockSpec((D, 4 * ATT_W), const2, pipeline_mode=pl.Buffered(1)),
            pl.BlockSpec((1, ATT_W), const2),
            pl.BlockSpec((1, ATT_W), const2),
        ],
        out_specs=[
            row_blk,
            row_blk,
            pl.BlockSpec((1, 1, ATT_W, ts), lambda b, i: (b, i, 0, 0)),
            row_blk,
            pl.BlockSpec((1, 1, 1, ATT_W), lambda b, i: (b, i, 0, 0)),
        ],
        out_shape=[
            jax.ShapeDtypeStruct((B, S, ATT_W), BF16),
            jax.ShapeDtypeStruct((B, S, ATT_W), BF16),
            jax.ShapeDtypeStruct((B, nb, ATT_W, ts), BF16),
            jax.ShapeDtypeStruct((B, S, ATT_W), BF16),
            jax.ShapeDtypeStruct((B, nb, 1, ATT_W), F32),
        ],
        compiler_params=pltpu.CompilerParams(
            dimension_semantics=("arbitrary", "arbitrary"), vmem_limit_bytes=VMEM_LIMIT),
        name="moba_proj",
    )(x, norm_g.reshape(1, D), w_in.astype(BF16),
      jnp.tile(q_norm_g, N_HEADS).reshape(1, ATT_W), jnp.tile(k_norm_g, N_HEADS).reshape(1, ATT_W))


def _moba_attn_kernel(slopes_ref, q_lo_ref, q_hi_ref, k_ref, vt_ref, km_ref, gz_lo_ref, gz_hi_ref,
                      o_lo_ref, o_hi_ref, qa_ref, bias_ref, *head_scratch):
    blk = MOBA_BLOCK
    n_h = ATTN_HEADS_PER_STEP
    n_sets = ATTN_SCRATCH_SETS
    s_refs = [head_scratch[h % n_sets] for h in range(n_h)]
    p_refs = [head_scratch[n_sets + h % n_sets] for h in range(n_h)]
    heads = range(n_h)

    def lanes_of(h):
        g0 = (h // HEADS_PER_GROUP) * LANES
        return slice(g0, g0 + LANES)
    nb = km_ref.shape[1]
    half = nb // 2
    p = pl.program_id(1)
    step = pl.program_id(2)
    neg_inf = jnp.float32(-jnp.inf)
    LO, HI = 0, 1
    q_blk = (step, nb - 1 - step)
    q_refs = (q_lo_ref, q_hi_ref)

    km = km_ref[0]
    row = lax.broadcasted_iota(jnp.int32, (LANES, 1), 0)
    lane = lax.broadcasted_iota(jnp.int32, (1, LANES), 1)
    blk_id = lax.broadcasted_iota(jnp.int32, (nb, blk), 0)
    key_off = lax.broadcasted_iota(jnp.int32, (blk, LANES), 0).astype(F32)
    k_extra = jnp.where(lane < 3, key_off, 0.0).astype(BF16)
    causal = (lax.broadcasted_iota(jnp.int32, (blk, blk), 1)
              >= lax.broadcasted_iota(jnp.int32, (blk, blk), 0))

    def fold8(t, op):
        parts = [t[r:r + SUBLANES] for r in range(0, t.shape[0], SUBLANES)]
        while len(parts) > 1:
            parts = [op(parts[a], parts[a + 1]) for a in range(0, len(parts), 2)]
        return parts[0]

    for h in heads:
        hh = h % HEADS_PER_GROUP
        cs = slopes_ref[p * n_h + h] * LOG2_E
        cs_v = jnp.full((LANES, blk), cs, F32)
        cs_hi = cs_v.astype(BF16).astype(F32)
        cs_mid = (cs_v - cs_hi).astype(BF16).astype(F32)
        cs_lo = cs_v - cs_hi - cs_mid
        q_extra = jnp.where(row == 0, cs_hi, jnp.where(row == 1, cs_mid, jnp.where(row == 2, cs_lo, 0.0)))
        q_extra = q_extra.astype(BF16)
        kmh = jnp.where(lane // HEAD_DIM == hh, km[:, lanes_of(h)], 0.0)
        km_hi = kmh.astype(BF16)
        km_lo = (kmh - km_hi.astype(F32)).astype(BF16)
        for qb in (LO, HI):
            i = q_blk[qb]
            q_t = q_refs[qb][0, :, lanes_of(h)].astype(F32).T
            q_head = jnp.where(row // HEAD_DIM == hh, q_t, 0.0).astype(BF16)
            qa_ref[qb, h] = jnp.concatenate([q_head, q_extra], axis=0)

            gate = _dot(km_hi, q_head) + _dot(km_lo, q_head)
            rank = jnp.zeros((nb, blk), jnp.int32)
            for mth in range(half if qb == LO else nb - 1):
                gm = gate[mth:mth + 1, :]
                beats = (gm > gate) | ((gm == gate) & (mth < blk_id))
                rank = rank + jnp.where(mth < i, jnp.where(beats, 1, 0), 0)
            keep = (rank < MOBA_TOPK) & (blk_id < i)
            dist = ((blk_id - i) * blk).astype(F32)
            bias_ref[qb, h] = jnp.where(keep, cs * dist, jnp.where(blk_id == i, 0.0, neg_inf))

    def key_rows(j):
        return pl.ds(pl.multiple_of(j * blk, blk), blk)

    n_slots = nb + 1
    slot_is_lo, slot_key, slot_causal = [], [], []
    for t in range(n_slots):
        if t <= half:
            slot_is_lo.append(None)
            slot_key.append(q_blk[HI] - half + t)
            slot_causal.append(t == half)
        else:
            u = t - (half + 1)
            is_lo = u >= q_blk[HI] - half
            slot_is_lo.append(is_lo)
            slot_key.append(jnp.where(is_lo, u - (q_blk[HI] - half), u))
            slot_causal.append(t == n_slots - 1)

    def pick(is_lo, lo, hi):
        return hi if is_lo is None else jnp.where(is_lo, lo, hi)

    def bias_row(t, hh):
        qb = HI if slot_is_lo[t] is None else jnp.where(slot_is_lo[t], LO, HI)
        return bias_ref[qb, hh, pl.ds(slot_key[t], 1), :]


    m8 = [[jnp.full((SUBLANES, blk), neg_inf, F32) for _ in heads] for _ in (LO, HI)]
    m_row = [[None] * n_h for _ in (LO, HI)]
    acc = [[None] * n_h for _ in (LO, HI)]
    norm = [[None] * n_h for _ in (LO, HI)]
    n_fixed = half + 1

    def pass1(t, hh):
        is_lo = slot_is_lo[t]
        k_aug = jnp.concatenate([k_ref[0, key_rows(slot_key[t]), lanes_of(hh)], k_extra], axis=1)
        qa = qa_ref[HI, hh] if is_lo is None else qa_ref[jnp.where(is_lo, LO, HI), hh]
        s = _dot(k_aug, qa)
        if slot_causal[t]:
            s = jnp.where(causal, s, neg_inf)
        s_refs[hh][t] = s
        top = fold8(s, jnp.maximum) + bias_row(t, hh)
        if is_lo is None:
            m8[HI][hh] = jnp.maximum(m8[HI][hh], top)
        else:
            m8[LO][hh] = jnp.maximum(m8[LO][hh], jnp.where(is_lo, top, neg_inf))
            m8[HI][hh] = jnp.maximum(m8[HI][hh], jnp.where(is_lo, neg_inf, top))

    def row_max(hh):
        for qb in (LO, HI):
            m_row[qb][hh] = jnp.max(m8[qb][hh], axis=0, keepdims=True)

    def pass2(t, hh):
        shift = pick(slot_is_lo[t], m_row[LO][hh], m_row[HI][hh]) - bias_row(t, hh)
        p_refs[hh][t] = jnp.exp2(s_refs[hh][t] - shift).astype(BF16)

    ones_rows = jnp.ones((BF16_ROWS, blk), F32)
    v_rows = HEAD_DIM + BF16_ROWS

    def values_and_ones(t, hh):
        return jnp.concatenate(
            [vt_ref[0, slot_key[t], hh * HEAD_DIM:(hh + 1) * HEAD_DIM, :].astype(F32), ones_rows], axis=0)

    pv_fixed = [None] * n_h

    def value_product_fixed(hh):
        v_fixed = jnp.concatenate([values_and_ones(t, hh).astype(BF16) for t in range(n_fixed)], axis=1)
        pv_fixed[hh] = _dot(v_fixed, p_refs[hh][0:n_fixed].reshape(n_fixed * blk, blk))

    def value_product_split(hh):
        pieces = []
        for t in range(n_fixed, n_slots):
            v32 = values_and_ones(t, hh)
            pieces.append(jnp.concatenate([jnp.where(slot_is_lo[t], v32, 0.0),
                                           jnp.where(slot_is_lo[t], 0.0, v32)], axis=0).astype(BF16))
        v_split = jnp.concatenate(pieces, axis=1)
        pv_split = _dot(v_split, p_refs[hh][n_fixed:n_slots].reshape((n_slots - n_fixed) * blk, blk))
        pv_hi = pv_fixed[hh] + pv_split[v_rows:2 * v_rows]
        acc[LO][hh] = pv_split[0:HEAD_DIM]
        norm[LO][hh] = pv_split[HEAD_DIM:HEAD_DIM + 1]
        acc[HI][hh] = pv_hi[0:HEAD_DIM]
        norm[HI][hh] = pv_hi[HEAD_DIM:HEAD_DIM + 1]

    for t in range(n_slots):
        pass1(t, 0)
    row_max(0)
    for h in heads:
        for t in range(n_slots):
            pass2(t, h)
            if h + 1 < n_h:
                pass1(t, h + 1)
            if t == 2 and h > 0:
                value_product_split(h - 1)
            if t == n_fixed:
                value_product_fixed(h)
        if h + 1 < n_h:
            row_max(h + 1)
    value_product_split(n_h - 1)

    for qb, o_ref, gz_ref in ((LO, o_lo_ref, gz_lo_ref), (HI, o_hi_ref, gz_hi_ref)):
        o_t = jnp.concatenate([acc[qb][hh] / norm[qb][hh] for hh in heads], axis=0)
        o_ref[0] = (o_t.T * gz_ref[0].astype(F32)).astype(BF16)


def _moba_attn(q, k, vt, kmean, gz, slopes):
    B, S, _ = q.shape
    blk = MOBA_BLOCK
    nb = S // blk
    n_h = ATTN_HEADS_PER_STEP
    width = n_h * HEAD_DIM
    half = nb // 2
    lo_blk = pl.BlockSpec((1, blk, width), lambda b, p, s: (b, s, p))
    hi_blk = pl.BlockSpec((1, blk, width), lambda b, p, s: (b, nb - 1 - s, p))
    return pl.pallas_call(
        _moba_attn_kernel,
        grid=(B, ATT_W // width, half),
        in_specs=[
            pl.BlockSpec(memory_space=pltpu.SMEM),
            lo_blk,
            hi_blk,
            pl.BlockSpec((1, S, width), lambda b, p, s: (b, 0, p)),
            pl.BlockSpec((1, nb, width, blk), lambda b, p, s: (b, 0, p, 0)),
            pl.BlockSpec((1, nb, width), lambda b, p, s: (b, 0, p)),
            lo_blk,
            hi_blk,
        ],
        out_specs=[
            lo_blk,
            pl.BlockSpec((1, blk, width), lambda b, p, s: (b, half - 1 - s, p)),
        ],
        out_shape=[jax.ShapeDtypeStruct((B, S // 2, ATT_W), BF16)] * 2,
        scratch_shapes=[
            pltpu.VMEM((2, n_h, 2 * LANES, blk), BF16),
            pltpu.VMEM((2, n_h, nb, blk), F32),
        ] + [pltpu.VMEM((nb + 1, blk, blk), F32)] * ATTN_SCRATCH_SETS
          + [pltpu.VMEM((nb + 1, blk, blk), BF16)] * ATTN_SCRATCH_SETS,
        compiler_params=pltpu.CompilerParams(
            dimension_semantics=("arbitrary", "arbitrary", "arbitrary"), vmem_limit_bytes=VMEM_LIMIT),
        name="moba_attn",
    )(slopes, q, q, k, vt, kmean, gz, gz)


def _moba_out_kernel(x_ref, og_lo_ref, og_hi_ref, w_ref, o_ref, *, tiles_per_half):
    in_lo = pl.program_id(1) < tiles_per_half

    @pl.when(in_lo)
    def _():
        o_ref[0] = x_ref[0] + _dot(og_lo_ref[0], w_ref[...])

    @pl.when(jnp.logical_not(in_lo))
    def _():
        o_ref[0] = x_ref[0] + _dot(og_hi_ref[0], w_ref[...])


def _moba_out(x, og_lo, og_hi, w_out):
    B, S, D = x.shape
    ts = 2 * SEQ_TILE
    per_half = S // 2 // ts
    return pl.pallas_call(
        functools.partial(_moba_out_kernel, tiles_per_half=per_half),
        grid=(B, S // ts),
        in_specs=[
            pl.BlockSpec((1, ts, D), lambda b, i: (b, i, 0)),
            pl.BlockSpec((1, ts, ATT_W), lambda b, i: (b, jnp.minimum(i, per_half - 1), 0)),
            pl.BlockSpec((1, ts, ATT_W), lambda b, i: (b, jnp.maximum(i - per_half, 0), 0)),
            pl.BlockSpec((ATT_W, D), lambda b, i: (0, 0), pipeline_mode=pl.Buffered(1)),
        ],
        out_specs=pl.BlockSpec((1, ts, D), lambda b, i: (b, i, 0)),
        out_shape=jax.ShapeDtypeStruct((B, S, D), F32),
        compiler_params=pltpu.CompilerParams(
            dimension_semantics=("arbitrary", "arbitrary"), vmem_limit_bytes=VMEM_LIMIT),
        name="moba_out",
    )(x, og_lo, og_hi, w_out.astype(BF16))


def kernel(x, norm0_g, w_in0, pool_w, pool_scale, conv_w, conv_b, cnorm_g, cnorm_b, w_out0,
           norm1_g, w_in1, q_norm_g, k_norm_g, w_out1):
    assert x.shape[1] % MOBA_BLOCK == 0 and x.shape[2] == D_MODEL
    x = _pool_conv_layer(x, norm0_g[0], w_in0[0], pool_w[0], pool_scale[0], conv_w[0], conv_b[0],
                         cnorm_g[0], cnorm_b[0], w_out0[0])
    q, k, vt, gz, kmean = _moba_proj(x, norm1_g[0], w_in1[0], q_norm_g[0], k_norm_g[0])
    slopes = jnp.exp2(-8.0 * jnp.arange(1, N_HEADS + 1, dtype=F32) / N_HEADS)
    og_lo, og_hi = _moba_attn(q, k, vt, kmean.reshape(kmean.shape[0], -1, ATT_W), gz, slopes)
    return _moba_out(x, og_lo, og_hi, w_out1[0])
```

```python
import functools
import math

import jax
import jax.numpy as jnp
from jax import lax
from jax.experimental import pallas as pl
from jax.experimental.pallas import tpu as pltpu

D_MODEL = 1024
A_W = 1024
B_W = 1024
MIX_W = 2048
POOL_WINDOWS = (2, 4, 8, 16)
POOL_CH = 256
CONV_K = 31
IN0_W = A_W + 2 * B_W + MIX_W
HEAD_DIM = 64
N_HEADS = 16
ATT_W = 1024
MOBA_BLOCK = 256
MOBA_TOPK = 3
EPS = 1e-6
LOG2_E = math.log2(math.e)

SEQ_TILE = 256
EW_ROWS = 32
CONV_ROWS = 64
CONV_COLS = 256
LANES = 128
SUBLANES = 8
BF16_ROWS = 16
HEADS_PER_GROUP = LANES // HEAD_DIM
ATTN_HEADS_PER_STEP = 8
ATTN_SCRATCH_SETS = 3
VMEM_LIMIT = 56 * 1024 * 1024

F32 = jnp.float32
BF16 = jnp.bfloat16


def _dot(a, b):
    return jnp.dot(a, b, preferred_element_type=F32)


def _rms_norm_rows(x, g):
    ms = jnp.mean(x * x, axis=-1, keepdims=True)
    return x * lax.rsqrt(ms + EPS) * g


def _sigmoid(x):
    return 1.0 / (1.0 + jnp.exp(-x))


def _pool_conv_kernel(x_ref, ng_ref, win_ref, pw_ref, ps_ref, cw_ref, cb_ref, cg_ref, cbeta_ref,
                      wout_ref, o_ref, a_buf, u_buf, h_buf, hp_buf, proj_buf, pool_buf, mix_buf, conv_buf,
                      y_buf, ynat_buf):
    ts = SEQ_TILE
    strand = ts // SUBLANES
    i = pl.program_id(1)

    r0 = lax.broadcasted_iota(jnp.int32, (ts, ts), 0)
    r1 = lax.broadcasted_iota(jnp.int32, (ts, ts), 1)
    to_perm = jnp.where(r1 == (r0 % SUBLANES) * strand + r0 // SUBLANES, 1.0, 0.0).astype(BF16)
    from_perm = jnp.where(r0 == (r1 % SUBLANES) * strand + r1 // SUBLANES, 1.0, 0.0).astype(BF16)
    row = lax.broadcasted_iota(jnp.int32, (ts, 1), 0)
    first_strand = row % SUBLANES == 0
    tpos = i * ts + (row % SUBLANES) * strand + row // SUBLANES

    @pl.when(i == 0)
    def _():
        a_buf[ts:2 * ts, :] = jnp.zeros((ts, A_W), F32)
        u_buf[ts:2 * ts, :] = jnp.zeros((ts, B_W), F32)

    def last_strand_to_front(buf, cols):
        buf[0:ts, cols] = pltpu.roll(buf[ts:2 * ts, cols], ts - (SUBLANES - 1), 0)

    def finish_shift(buf, cols):
        buf[0:ts, cols] = jnp.where(first_strand, buf[0:ts, cols], pltpu.roll(buf[ts:2 * ts, cols], 1, 0))

    def delayed(buf, d, rows, cols):
        start = (strand - d) * SUBLANES + rows.start
        return buf[start:start + (rows.stop - rows.start), cols]

    def row_chunks(n):
        return [slice(r, r + n) for r in range(0, ts, n)]

    def shifted_cols(cols, by):
        return slice(cols.start + by, cols.stop + by)

    def silu_z(rows, cols):
        z = proj_buf[rows, shifted_cols(cols, 2 * B_W)]
        return z * _sigmoid(z)

    col_chunks = [slice(c, c + CONV_COLS) for c in range(0, B_W, CONV_COLS)]
    conv_rows = row_chunks(CONV_ROWS)
    ew_rows = row_chunks(EW_ROWS)
    matmul_steps, vector_steps = [], []

    def step(queue, name, cost, needs, fn):
        queue.append((name, cost, tuple(needs), fn))

    def in_proj(c0):
        def run():
            proj_buf[:, c0:c0 + CONV_COLS] = _dot(hp_buf[...], win_ref[:, A_W + c0:A_W + c0 + CONV_COLS])
        return run

    n_z = MIX_W // CONV_COLS
    for c, cols in enumerate(col_chunks):
        step(matmul_steps, ("bv", c), 256, [], in_proj(cols.start))
        step(matmul_steps, ("bg", c), 256, [], in_proj(B_W + cols.start))
    for c, cols in enumerate(col_chunks):
        def pool_in(cols=cols):
            last_strand_to_front(a_buf, cols)
            a_buf[ts:2 * ts, cols] = _dot(hp_buf[...], win_ref[:, cols])
            finish_shift(a_buf, cols)
        step(matmul_steps, ("a", c), 256, [], pool_in)
    for j in range(n_z):
        step(matmul_steps, ("z", j), 256, [], in_proj(2 * B_W + j * CONV_COLS))
    for g in range(len(POOL_WINDOWS)):
        def pool_mix(g=g):
            gc = col_chunks[g]
            mix_buf[:, gc] = _dot(pool_buf[:, gc], pw_ref[g])
        step(matmul_steps, ("pool_mix", g), 64, [("pool", g, r) for r in range(len(conv_rows))], pool_mix)

    def unpermute(cols):
        def run():
            ynat_buf[:, cols] = _dot(from_perm, y_buf[:, cols]).astype(BF16)
        return run

    pool_cols, conv_cols = slice(0, A_W), slice(A_W, MIX_W)

    def out_first():
        o_ref[0] = x_ref[0] + _dot(ynat_buf[:, pool_cols], wout_ref[pool_cols, :])

    def out_second():
        o_ref[0] += _dot(ynat_buf[:, conv_cols], wout_ref[conv_cols, :])

    step(matmul_steps, "unperm_pool", 256, [("pool_gate", r) for r in range(len(ew_rows))], unpermute(pool_cols))
    step(matmul_steps, "out_pool", 1024, ["unperm_pool"], out_first)
    step(matmul_steps, "unperm_conv", 256, [("norm_gate", r) for r in range(len(ew_rows))], unpermute(conv_cols))
    step(matmul_steps, "out_conv", 1024, ["unperm_conv", "out_pool"], out_second)

    for c, cols in enumerate(col_chunks):
        def glu(cols=cols):
            last_strand_to_front(u_buf, cols)
            for rows in conv_rows:
                u_buf[ts + rows.start:ts + rows.stop, cols] = (
                    proj_buf[rows, cols] * _sigmoid(proj_buf[rows, shifted_cols(cols, B_W)]))
            finish_shift(u_buf, cols)
        step(vector_steps, ("glu", c), 200, [("bv", c), ("bg", c)], glu)
        for r, rows in enumerate(conv_rows):
            def conv(rows=rows, cols=cols):
                acc = jnp.broadcast_to(cb_ref[:, cols], (CONV_ROWS, CONV_COLS))
                for k in range(CONV_K):
                    acc = acc + cw_ref[k:k + 1, cols] * delayed(u_buf, CONV_K - 1 - k, rows, cols)
                conv_buf[rows, cols] = acc
            step(vector_steps, ("conv", c, r), 250, [("glu", c)], conv)

    for g, w in enumerate(POOL_WINDOWS):
        for r, rows in enumerate(conv_rows):
            def pool(g=g, w=w, rows=rows):
                cols = col_chunks[g]
                cur = delayed(a_buf, 0, rows, cols)
                win = cur
                for d in range(1, w):
                    win = win + delayed(a_buf, d, rows, cols)
                cnt = jnp.minimum(tpos[rows] + 1, w).astype(F32)
                pool_buf[rows, cols] = (win / cnt - cur).astype(BF16)
            step(vector_steps, ("pool", g, r), 30, [("a", g)], pool)
    for r, rows in enumerate(ew_rows):
        def pool_gate(rows=rows):
            y_buf[rows, pool_cols] = (mix_buf[rows, :] * ps_ref[...] * silu_z(rows, pool_cols)).astype(BF16)
        needs = [("pool_mix", g) for g in range(len(POOL_WINDOWS))] + [("z", j) for j in range(n_z // 2)]
        step(vector_steps, ("pool_gate", r), 80, needs, pool_gate)

    for r, rows in enumerate(ew_rows):
        def norm_gate(rows=rows):
            conv = conv_buf[rows, :]
            mu = jnp.mean(conv, axis=-1, keepdims=True)
            cen = conv - mu
            var = jnp.mean(cen * cen, axis=-1, keepdims=True)
            yn = cen * lax.rsqrt(var + EPS) * cg_ref[...] + cbeta_ref[...]
            y_buf[rows, conv_cols] = (yn * _sigmoid(yn) * silu_z(rows, shifted_cols(pool_cols, A_W))).astype(BF16)
        needs = ([("conv", c, q) for c in range(len(col_chunks)) for q in range(len(conv_rows))]
                 + [("z", j) for j in range(n_z // 2, n_z)])
        step(vector_steps, ("norm_gate", r), 190, needs, norm_gate)

    def emit(queues):
        done, spent, pos = set(), [0] * len(queues), [0] * len(queues)
        while any(pos[q] < len(queues[q]) for q in range(len(queues))):
            ready = [q for q in range(len(queues)) if pos[q] < len(queues[q])
                     and all(n in done for n in queues[q][pos[q]][2])]
            assert ready, "step ordering deadlock"
            q = min(ready, key=lambda q: spent[q])
            name, cost, _, fn = queues[q][pos[q]]
            fn()
            done.add(name)
            spent[q] += cost
            pos[q] += 1

    for rows in ew_rows:
        h_buf[rows, :] = _rms_norm_rows(x_ref[0, rows, :], ng_ref[...]).astype(BF16)
    hp_buf[...] = _dot(to_perm, h_buf[...]).astype(BF16)
    emit([matmul_steps, vector_steps])


def _pool_conv_layer(x, norm_g, w_in, pool_w, pool_scale, conv_w, conv_b, cn_g, cn_b, w_out):
    B, S, D = x.shape
    ts = SEQ_TILE
    const2 = lambda b, i: (0, 0)
    const3 = lambda b, i: (0, 0, 0)
    single = pl.Buffered(1)
    return pl.pallas_call(
        _pool_conv_kernel,
        grid=(B, S // ts),
        in_specs=[
            pl.BlockSpec((1, ts, D), lambda b, i: (b, i, 0)),
            pl.BlockSpec((1, D), const2),
            pl.BlockSpec((D, IN0_W), const2, pipeline_mode=single),
            pl.BlockSpec((len(POOL_WINDOWS), POOL_CH, POOL_CH), const3, pipeline_mode=single),
            pl.BlockSpec((1, A_W), const2),
            pl.BlockSpec((CONV_K, B_W), const2),
            pl.BlockSpec((1, B_W), const2),
            pl.BlockSpec((1, B_W), const2),
            pl.BlockSpec((1, B_W), const2),
            pl.BlockSpec((MIX_W, D), const2, pipeline_mode=single),
        ],
        out_specs=pl.BlockSpec((1, ts, D), lambda b, i: (b, i, 0)),
        out_shape=jax.ShapeDtypeStruct((B, S, D), F32),
        scratch_shapes=[
            pltpu.VMEM((2 * ts, A_W), F32),
            pltpu.VMEM((2 * ts, B_W), F32),
            pltpu.VMEM((ts, D), BF16),
            pltpu.VMEM((ts, D), BF16),
            pltpu.VMEM((ts, IN0_W - A_W), F32),
            pltpu.VMEM((ts, A_W), BF16),
            pltpu.VMEM((ts, A_W), F32),
            pltpu.VMEM((ts, B_W), F32),
            pltpu.VMEM((ts, MIX_W), BF16),
            pltpu.VMEM((ts, MIX_W), BF16),
        ],
        compiler_params=pltpu.CompilerParams(
            dimension_semantics=("arbitrary", "arbitrary"), vmem_limit_bytes=VMEM_LIMIT),
        name="pool_conv_layer",
    )(x, norm_g.reshape(1, D), w_in.astype(BF16), pool_w.astype(BF16), pool_scale.reshape(1, A_W),
      conv_w, conv_b.reshape(1, B_W), cn_g.reshape(1, B_W), cn_b.reshape(1, B_W), w_out.astype(BF16))


def _head_rms_norm(t, g_row):
    low = lax.broadcasted_iota(jnp.int32, (1, LANES), 1) < HEAD_DIM
    cols = []
    for c in range(ATT_W // LANES):
        tc = t[:, c * LANES:(c + 1) * LANES]
        sq = tc * tc
        s_low = jnp.sum(jnp.where(low, sq, 0.0), axis=-1, keepdims=True)
        s_high = jnp.sum(jnp.where(low, 0.0, sq), axis=-1, keepdims=True)
        ms = jnp.where(low, s_low, s_high) * (1.0 / HEAD_DIM)
        cols.append(tc * lax.rsqrt(ms + EPS) * g_row[:, c * LANES:(c + 1) * LANES])
    return cols


def _moba_proj_kernel(x_ref, ng_ref, win_ref, qg_ref, kg_ref, q_ref, k_ref, vt_ref, gz_ref, km_ref):
    x = x_ref[0]
    h = _rms_norm_rows(x, ng_ref[...]).astype(BF16)

    q = _dot(h, win_ref[:, 0:ATT_W])
    for c, qc in enumerate(_head_rms_norm(q, qg_ref[...])):
        q_ref[0, :, c * LANES:(c + 1) * LANES] = (qc * (HEAD_DIM ** -0.5 * LOG2_E)).astype(BF16)

    k = _dot(h, win_ref[:, ATT_W:2 * ATT_W])
    for c, kc in enumerate(_head_rms_norm(k, kg_ref[...])):
        k_ref[0, :, c * LANES:(c + 1) * LANES] = kc.astype(BF16)
        km_ref[0, 0, :, c * LANES:(c + 1) * LANES] = jnp.mean(kc, axis=0, keepdims=True)

    v = _dot(h, win_ref[:, 2 * ATT_W:3 * ATT_W])
    vt_ref[0, 0] = v.T.astype(BF16)

    z = _dot(h, win_ref[:, 3 * ATT_W:4 * ATT_W])
    gz_ref[0] = (z * _sigmoid(z)).astype(BF16)


def _moba_proj(x, norm_g, w_in, q_norm_g, k_norm_g):
    B, S, D = x.shape
    ts = MOBA_BLOCK
    nb = S // ts
    const2 = lambda b, i: (0, 0)
    row_blk = pl.BlockSpec((1, ts, ATT_W), lambda b, i: (b, i, 0))
    return pl.pallas_call(
        _moba_proj_kernel,
        grid=(B, nb),
        in_specs=[
            pl.BlockSpec((1, ts, D), lambda b, i: (b, i, 0)),
            pl.BlockSpec((1, D), const2),
            pl.BlockSpec((D, 4 * ATT_W), const2, pipeline_mode=pl.Buffered(1)),
            pl.BlockSpec((1, ATT_W), const2),
            pl.BlockSpec((1, ATT_W), const2),
        ],
        out_specs=[
            row_blk,
            row_blk,
            pl.BlockSpec((1, 1, ATT_W, ts), lambda b, i: (b, i, 0, 0)),
            row_blk,
            pl.BlockSpec((1, 1, 1, ATT_W), lambda b, i: (b, i, 0, 0)),
        ],
        out_shape=[
            jax.ShapeDtypeStruct((B, S, ATT_W), BF16),
            jax.ShapeDtypeStruct((B, S, ATT_W), BF16),
            jax.ShapeDtypeStruct((B, nb, ATT_W, ts), BF16),
            jax.ShapeDtypeStruct((B, S, ATT_W), BF16),
            jax.ShapeDtypeStruct((B, nb, 1, ATT_W), F32),
        ],
        compiler_params=pltpu.CompilerParams(
            dimension_semantics=("arbitrary", "arbitrary"), vmem_limit_bytes=VMEM_LIMIT),
        name="moba_proj",
    )(x, norm_g.reshape(1, D), w_in.astype(BF16),
      jnp.tile(q_norm_g, N_HEADS).reshape(1, ATT_W), jnp.tile(k_norm_g, N_HEADS).reshape(1, ATT_W))


def _moba_attn_kernel(slopes_ref, q_lo_ref, q_hi_ref, k_ref, vt_ref, km_ref, gz_lo_ref, gz_hi_ref,
                      o_lo_ref, o_hi_ref, qa_ref, bias_ref, *head_scratch):
    blk = MOBA_BLOCK
    n_h = ATTN_HEADS_PER_STEP
    n_sets = ATTN_SCRATCH_SETS
    s_refs = [head_scratch[h % n_sets] for h in range(n_h)]
    p_refs = [head_scratch[n_sets + h % n_sets] for h in range(n_h)]
    heads = range(n_h)

    def lanes_of(h):
        g0 = (h // HEADS_PER_GROUP) * LANES
        return slice(g0, g0 + LANES)
    nb = km_ref.shape[1]
    half = nb // 2
    p = pl.program_id(1)
    step = pl.program_id(2)
    neg_inf = jnp.float32(-jnp.inf)
    LO, HI = 0, 1
    q_blk = (step, nb - 1 - step)
    q_refs = (q_lo_ref, q_hi_ref)

    km = km_ref[0]
    row = lax.broadcasted_iota(jnp.int32, (LANES, 1), 0)
    lane = lax.broadcasted_iota(jnp.int32, (1, LANES), 1)
    blk_id = lax.broadcasted_iota(jnp.int32, (nb, blk), 0)
    blk_f = blk_id.astype(F32)
    key_off = lax.broadcasted_iota(jnp.int32, (blk, LANES), 0).astype(F32)
    k_extra = jnp.where(lane < 3, key_off, 0.0).astype(BF16)
    causal = (lax.broadcasted_iota(jnp.int32, (blk, blk), 1)
              >= lax.broadcasted_iota(jnp.int32, (blk, blk), 0))

    def fold8(t, op):
        parts = [t[r:r + SUBLANES] for r in range(0, t.shape[0], SUBLANES)]
        while len(parts) > 1:
            parts = [op(parts[a], parts[a + 1]) for a in range(0, len(parts), 2)]
        return parts[0]

    cslope = []
    for h in heads:
        hh = h % HEADS_PER_GROUP
        cs = slopes_ref[p * n_h + h] * LOG2_E
        cslope.append(cs)
        cs_v = jnp.full((LANES, blk), cs, F32)
        cs_hi = cs_v.astype(BF16).astype(F32)
        cs_mid = (cs_v - cs_hi).astype(BF16).astype(F32)
        cs_lo = cs_v - cs_hi - cs_mid
        q_extra = jnp.where(row == 0, cs_hi, jnp.where(row == 1, cs_mid, jnp.where(row == 2, cs_lo, 0.0)))
        q_extra = q_extra.astype(BF16)
        for qb in (LO, HI):
            q_t = q_refs[qb][0, :, lanes_of(h)].astype(F32).T
            q_head = jnp.where(row // HEAD_DIM == hh, q_t, 0.0).astype(BF16)
            qa_ref[qb, h] = jnp.concatenate([q_head, q_extra], axis=0)

    def choose_blocks(h, qb):
        i = q_blk[qb]
        kmh = jnp.where(lane // HEAD_DIM == h % HEADS_PER_GROUP, km[:, lanes_of(h)], 0.0)
        km_hi = kmh.astype(BF16)
        km_lo = (kmh - km_hi.astype(F32)).astype(BF16)
        q_head = qa_ref[qb, h, 0:LANES, :]
        gate = _dot(km_hi, q_head) + _dot(km_lo, q_head)
        avail = jnp.where(blk_id < i, 1.0, 0.0)
        keep = jnp.zeros((nb, blk), F32)
        for _ in range(MOBA_TOPK):
            g = jnp.where(avail > 0.0, gate, neg_inf)
            top = jnp.max(g, axis=0, keepdims=True)
            first = jnp.min(jnp.where((g == top) & (avail > 0.0), blk_f, float(nb)), axis=0, keepdims=True)
            pick = jnp.where(blk_f == first, 1.0, 0.0)
            keep = keep + pick
            avail = avail - pick
        dist = ((blk_id - i) * blk).astype(F32)
        bias_ref[qb, h] = jnp.where(keep > 0.0, cslope[h] * dist, jnp.where(blk_id == i, 0.0, neg_inf))

    def key_rows(j):
        return pl.ds(pl.multiple_of(j * blk, blk), blk)

    n_slots = nb + 1
    slot_is_lo, slot_key, slot_causal = [], [], []
    for t in range(n_slots):
        if t <= half:
            slot_is_lo.append(None)
            slot_key.append(q_blk[HI] - half + t)
            slot_causal.append(t == half)
        else:
            u = t - (half + 1)
            is_lo = u >= q_blk[HI] - half
            slot_is_lo.append(is_lo)
            slot_key.append(jnp.where(is_lo, u - (q_blk[HI] - half), u))
            slot_causal.append(t == n_slots - 1)

    def pick(is_lo, lo, hi):
        return hi if is_lo is None else jnp.where(is_lo, lo, hi)

    def bias_row(t, hh):
        qb = HI if slot_is_lo[t] is None else jnp.where(slot_is_lo[t], LO, HI)
        return bias_ref[qb, hh, pl.ds(slot_key[t], 1), :]


    m8 = [[jnp.full((SUBLANES, blk), neg_inf, F32) for _ in heads] for _ in (LO, HI)]
    m_row = [[None] * n_h for _ in (LO, HI)]
    acc = [[None] * n_h for _ in (LO, HI)]
    norm = [[None] * n_h for _ in (LO, HI)]
    n_fixed = half + 1

    def pass1(t, hh):
        is_lo = slot_is_lo[t]
        k_aug = jnp.concatenate([k_ref[0, key_rows(slot_key[t]), lanes_of(hh)], k_extra], axis=1)
        qa = qa_ref[HI, hh] if is_lo is None else qa_ref[jnp.where(is_lo, LO, HI), hh]
        s = _dot(k_aug, qa)
        if slot_causal[t]:
            s = jnp.where(causal, s, neg_inf)
        s_refs[hh][t] = s
        top = fold8(s, jnp.maximum) + bias_row(t, hh)
        if is_lo is None:
            m8[HI][hh] = jnp.maximum(m8[HI][hh], top)
        else:
            m8[LO][hh] = jnp.maximum(m8[LO][hh], jnp.where(is_lo, top, neg_inf))
            m8[HI][hh] = jnp.maximum(m8[HI][hh], jnp.where(is_lo, neg_inf, top))

    def row_max(hh):
        for qb in (LO, HI):
            m_row[qb][hh] = jnp.max(m8[qb][hh], axis=0, keepdims=True)

    def pass2(t, hh):
        shift = pick(slot_is_lo[t], m_row[LO][hh], m_row[HI][hh]) - bias_row(t, hh)
        p_refs[hh][t] = jnp.exp2(s_refs[hh][t] - shift).astype(BF16)

    ones_rows = jnp.ones((BF16_ROWS, blk), F32)
    v_rows = HEAD_DIM + BF16_ROWS

    def values_and_ones(t, hh):
        return jnp.concatenate(
            [vt_ref[0, slot_key[t], hh * HEAD_DIM:(hh + 1) * HEAD_DIM, :].astype(F32), ones_rows], axis=0)

    pv_fixed = [None] * n_h

    def value_product_fixed(hh):
        v_fixed = jnp.concatenate([values_and_ones(t, hh).astype(BF16) for t in range(n_fixed)], axis=1)
        pv_fixed[hh] = _dot(v_fixed, p_refs[hh][0:n_fixed].reshape(n_fixed * blk, blk))

    def value_product_split(hh):
        pieces = []
        for t in range(n_fixed, n_slots):
            v32 = values_and_ones(t, hh)
            pieces.append(jnp.concatenate([jnp.where(slot_is_lo[t], v32, 0.0),
                                           jnp.where(slot_is_lo[t], 0.0, v32)], axis=0).astype(BF16))
        v_split = jnp.concatenate(pieces, axis=1)
        pv_split = _dot(v_split, p_refs[hh][n_fixed:n_slots].reshape((n_slots - n_fixed) * blk, blk))
        pv_hi = pv_fixed[hh] + pv_split[v_rows:2 * v_rows]
        acc[LO][hh] = pv_split[0:HEAD_DIM]
        norm[LO][hh] = pv_split[HEAD_DIM:HEAD_DIM + 1]
        acc[HI][hh] = pv_hi[0:HEAD_DIM]
        norm[HI][hh] = pv_hi[HEAD_DIM:HEAD_DIM + 1]

    for h in heads:
        for qb in (LO, HI):
            choose_blocks(h, qb)
    for t in range(n_slots):
        pass1(t, 0)
    row_max(0)
    for h in heads:
        for t in range(n_slots):
            pass2(t, h)
            if h + 1 < n_h:
                pass1(t, h + 1)
            if t == 2 and h > 0:
                value_product_split(h - 1)
            if t == n_fixed:
                value_product_fixed(h)
        if h + 1 < n_h:
            row_max(h + 1)
    value_product_split(n_h - 1)

    for qb, o_ref, gz_ref in ((LO, o_lo_ref, gz_lo_ref), (HI, o_hi_ref, gz_hi_ref)):
        o_t = jnp.concatenate([acc[qb][hh] / norm[qb][hh] for hh in heads], axis=0)
        o_ref[0] = (o_t.T * gz_ref[0].astype(F32)).astype(BF16)


def _moba_attn(q, k, vt, kmean, gz, slopes):
    B, S, _ = q.shape
    blk = MOBA_BLOCK
    nb = S // blk
    n_h = ATTN_HEADS_PER_STEP
    width = n_h * HEAD_DIM
    half = nb // 2
    lo_blk = pl.BlockSpec((1, blk, width), lambda b, p, s: (b, s, p))
    hi_blk = pl.BlockSpec((1, blk, width), lambda b, p, s: (b, nb - 1 - s, p))
    return pl.pallas_call(
        _moba_attn_kernel,
        grid=(B, ATT_W // width, half),
        in_specs=[
            pl.BlockSpec(memory_space=pltpu.SMEM),
            lo_blk,
            hi_blk,
            pl.BlockSpec((1, S, width), lambda b, p, s: (b, 0, p)),
            pl.BlockSpec((1, nb, width, blk), lambda b, p, s: (b, 0, p, 0)),
            pl.BlockSpec((1, nb, width), lambda b, p, s: (b, 0, p)),
            lo_blk,
            hi_blk,
        ],
        out_specs=[
            lo_blk,
            pl.BlockSpec((1, blk, width), lambda b, p, s: (b, half - 1 - s, p)),
        ],
        out_shape=[jax.ShapeDtypeStruct((B, S // 2, ATT_W), BF16)] * 2,
        scratch_shapes=[
            pltpu.VMEM((2, n_h, 2 * LANES, blk), BF16),
            pltpu.VMEM((2, n_h, nb, blk), F32),
        ] + [pltpu.VMEM((nb + 1, blk, blk), F32)] * ATTN_SCRATCH_SETS
          + [pltpu.VMEM((nb + 1, blk, blk), BF16)] * ATTN_SCRATCH_SETS,
        compiler_params=pltpu.CompilerParams(
            dimension_semantics=("arbitrary", "arbitrary", "arbitrary"), vmem_limit_bytes=VMEM_LIMIT),
        name="moba_attn",
    )(slopes, q, q, k, vt, kmean, gz, gz)


def _moba_out_kernel(x_ref, og_lo_ref, og_hi_ref, w_ref, o_ref, *, tiles_per_half):
    in_lo = pl.program_id(1) < tiles_per_half

    @pl.when(in_lo)
    def _():
        o_ref[0] = x_ref[0] + _dot(og_lo_ref[0], w_ref[...])

    @pl.when(jnp.logical_not(in_lo))
    def _():
        o_ref[0] = x_ref[0] + _dot(og_hi_ref[0], w_ref[...])


def _moba_out(x, og_lo, og_hi, w_out):
    B, S, D = x.shape
    ts = 4 * SEQ_TILE
    per_half = S // 2 // ts
    return pl.pallas_call(
        functools.partial(_moba_out_kernel, tiles_per_half=per_half),
        grid=(B, S // ts),
        in_specs=[
            pl.BlockSpec((1, ts, D), lambda b, i: (b, i, 0)),
            pl.BlockSpec((1, ts, ATT_W), lambda b, i: (b, jnp.minimum(i, per_half - 1), 0)),
            pl.BlockSpec((1, ts, ATT_W), lambda b, i: (b, jnp.maximum(i - per_half, 0), 0)),
            pl.BlockSpec((ATT_W, D), lambda b, i: (0, 0), pipeline_mode=pl.Buffered(1)),
        ],
        out_specs=pl.BlockSpec((1, ts, D), lambda b, i: (b, i, 0)),
        out_shape=jax.ShapeDtypeStruct((B, S, D), F32),
        compiler_params=pltpu.CompilerParams(
            dimension_semantics=("arbitrary", "arbitrary"), vmem_limit_bytes=VMEM_LIMIT),
        name="moba_out",
    )(x, og_lo, og_hi, w_out.astype(BF16))


def kernel(x, norm0_g, w_in0, pool_w, pool_scale, conv_w, conv_b, cnorm_g, cnorm_b, w_out0,
           norm1_g, w_in1, q_norm_g, k_norm_g, w_out1):
    assert x.shape[1] % MOBA_BLOCK == 0 and x.shape[2] == D_MODEL
    x = _pool_conv_layer(x, norm0_g[0], w_in0[0], pool_w[0], pool_scale[0], conv_w[0], conv_b[0],
                         cnorm_g[0], cnorm_b[0], w_out0[0])
    q, k, vt, gz, kmean = _moba_proj(x, norm1_g[0], w_in1[0], q_norm_g[0], k_norm_g[0])
    slopes = jnp.exp2(-8.0 * jnp.arange(1, N_HEADS + 1, dtype=F32) / N_HEADS)
    og_lo, og_hi = _moba_attn(q, k, vt, kmean.reshape(kmean.shape[0], -1, ATT_W), gz, slopes)
    return _moba_out(x, og_lo, og_hi, w_out1[0])
```

```python
import functools
import math

import jax
import jax.numpy as jnp
import numpy as np
from jax import lax
from jax.experimental import pallas as pl
from jax.experimental.pallas import tpu as pltpu

D_MODEL = 1024
A_W = 1024
B_W = 1024
MIX_W = 2048
POOL_WINDOWS = (2, 4, 8, 16)
POOL_CH = 256
CONV_K = 31
IN0_W = A_W + 2 * B_W + MIX_W
HEAD_DIM = 64
N_HEADS = 16
ATT_W = 1024
MOBA_BLOCK = 256
MOBA_TOPK = 3
EPS = 1e-6
LOG2_E = math.log2(math.e)

SEQ_TILE = 256
EW_ROWS = 32
CONV_ROWS = 64
CONV_COLS = 256
LANES = 128
SUBLANES = 8
BF16_ROWS = 16
HEADS_PER_GROUP = LANES // HEAD_DIM
ATTN_HEADS_PER_STEP = 8
ATTN_SCRATCH_SETS = 3
VMEM_LIMIT = 56 * 1024 * 1024

F32 = jnp.float32
BF16 = jnp.bfloat16


def _dot(a, b):
    return jnp.dot(a, b, preferred_element_type=F32)


def _rms_norm_rows(x, g):
    ms = jnp.mean(x * x, axis=-1, keepdims=True)
    return x * lax.rsqrt(ms + EPS) * g


def _sigmoid(x):
    return 1.0 / (1.0 + jnp.exp(-x))


def _strand_permutation(ts):
    r = np.arange(ts)
    time_of_row = (r % SUBLANES) * (ts // SUBLANES) + r // SUBLANES
    to_perm = (r[None, :] == time_of_row[:, None]).astype(np.float32)
    return jnp.asarray(np.stack([to_perm, to_perm.T]), BF16)


def _pool_conv_kernel(x_ref, perm_ref, ng_ref, win_ref, pw_ref, ps_ref, cw_ref, cb_ref, cg_ref, cbeta_ref,
                      wout_ref, o_ref, a_buf, u_buf, h_buf, hp_buf, proj_buf, pool_buf, mix_buf, conv_buf,
                      y_buf, ynat_buf):
    ts = SEQ_TILE
    strand = ts // SUBLANES
    i = pl.program_id(1)

    row = lax.broadcasted_iota(jnp.int32, (ts, 1), 0)
    first_strand = row % SUBLANES == 0
    tpos = i * ts + (row % SUBLANES) * strand + row // SUBLANES

    @pl.when(i == 0)
    def _():
        a_buf[ts:2 * ts, :] = jnp.zeros((ts, A_W), F32)
        u_buf[ts:2 * ts, :] = jnp.zeros((ts, B_W), F32)

    def last_strand_to_front(buf, cols):
        buf[0:ts, cols] = pltpu.roll(buf[ts:2 * ts, cols], ts - (SUBLANES - 1), 0)

    def finish_shift(buf, cols):
        buf[0:ts, cols] = jnp.where(first_strand, buf[0:ts, cols], pltpu.roll(buf[ts:2 * ts, cols], 1, 0))

    def delayed(buf, d, rows, cols):
        start = (strand - d) * SUBLANES + rows.start
        return buf[start:start + (rows.stop - rows.start), cols]

    def row_chunks(n):
        return [slice(r, r + n) for r in range(0, ts, n)]

    def shifted_cols(cols, by):
        return slice(cols.start + by, cols.stop + by)

    def silu_z(rows, cols):
        z = proj_buf[rows, shifted_cols(cols, 2 * B_W)]
        return z * _sigmoid(z)

    col_chunks = [slice(c, c + CONV_COLS) for c in range(0, B_W, CONV_COLS)]
    conv_rows = row_chunks(CONV_ROWS)
    ew_rows = row_chunks(EW_ROWS)
    matmul_steps, vector_steps = [], []

    def step(queue, name, cost, needs, fn):
        queue.append((name, cost, tuple(needs), fn))

    def in_proj(c0):
        def run():
            proj_buf[:, c0:c0 + CONV_COLS] = _dot(hp_buf[...], win_ref[:, A_W + c0:A_W + c0 + CONV_COLS])
        return run

    n_z = MIX_W // CONV_COLS
    for c, cols in enumerate(col_chunks):
        step(matmul_steps, ("bv", c), 256, [], in_proj(cols.start))
        step(matmul_steps, ("bg", c), 256, [], in_proj(B_W + cols.start))
    for c, cols in enumerate(col_chunks):
        def pool_in(cols=cols):
            last_strand_to_front(a_buf, cols)
            a_buf[ts:2 * ts, cols] = _dot(hp_buf[...], win_ref[:, cols])
            finish_shift(a_buf, cols)
        step(matmul_steps, ("a", c), 256, [], pool_in)
    for j in range(n_z):
        step(matmul_steps, ("z", j), 256, [], in_proj(2 * B_W + j * CONV_COLS))
    for g in range(len(POOL_WINDOWS)):
        def pool_mix(g=g):
            gc = col_chunks[g]
            mix_buf[:, gc] = _dot(pool_buf[:, gc], pw_ref[g])
        step(matmul_steps, ("pool_mix", g), 64, [("pool", g, r) for r in range(len(conv_rows))], pool_mix)

    def unpermute(cols):
        def run():
            ynat_buf[:, cols] = _dot(perm_ref[1], y_buf[:, cols]).astype(BF16)
        return run

    pool_cols, conv_cols = slice(0, A_W), slice(A_W, MIX_W)

    def out_first():
        o_ref[0] = x_ref[0] + _dot(ynat_buf[:, pool_cols], wout_ref[pool_cols, :])

    def out_second():
        o_ref[0] += _dot(ynat_buf[:, conv_cols], wout_ref[conv_cols, :])

    step(matmul_steps, "unperm_pool", 256, [("pool_gate", r) for r in range(len(ew_rows))], unpermute(pool_cols))
    step(matmul_steps, "out_pool", 1024, ["unperm_pool"], out_first)
    step(matmul_steps, "unperm_conv", 256, [("norm_gate", r) for r in range(len(ew_rows))], unpermute(conv_cols))
    step(matmul_steps, "out_conv", 1024, ["unperm_conv", "out_pool"], out_second)

    for c, cols in enumerate(col_chunks):
        def glu(cols=cols):
            last_strand_to_front(u_buf, cols)
            for rows in conv_rows:
                u_buf[ts + rows.start:ts + rows.stop, cols] = (
                    proj_buf[rows, cols] * _sigmoid(proj_buf[rows, shifted_cols(cols, B_W)]))
            finish_shift(u_buf, cols)
        step(vector_steps, ("glu", c), 200, [("bv", c), ("bg", c)], glu)
        for r, rows in enumerate(conv_rows):
            def conv(rows=rows, cols=cols):
                acc = jnp.broadcast_to(cb_ref[:, cols], (CONV_ROWS, CONV_COLS))
                for k in range(CONV_K):
                    acc = acc + cw_ref[k:k + 1, cols] * delayed(u_buf, CONV_K - 1 - k, rows, cols)
                conv_buf[rows, cols] = acc
            step(vector_steps, ("conv", c, r), 250, [("glu", c)], conv)

    for g, w in enumerate(POOL_WINDOWS):
        for r, rows in enumerate(conv_rows):
            def pool(g=g, w=w, rows=rows):
                cols = col_chunks[g]
                cur = delayed(a_buf, 0, rows, cols)
                win = cur
                for d in range(1, w):
                    win = win + delayed(a_buf, d, rows, cols)
                cnt = jnp.minimum(tpos[rows] + 1, w).astype(F32)
                pool_buf[rows, cols] = (win / cnt - cur).astype(BF16)
            step(vector_steps, ("pool", g, r), 30, [("a", g)], pool)
    for r, rows in enumerate(ew_rows):
        def pool_gate(rows=rows):
            y_buf[rows, pool_cols] = (mix_buf[rows, :] * ps_ref[...] * silu_z(rows, pool_cols)).astype(BF16)
        needs = [("pool_mix", g) for g in range(len(POOL_WINDOWS))] + [("z", j) for j in range(n_z // 2)]
        step(vector_steps, ("pool_gate", r), 80, needs, pool_gate)

    for r, rows in enumerate(ew_rows):
        def norm_gate(rows=rows):
            conv = conv_buf[rows, :]
            mu = jnp.mean(conv, axis=-1, keepdims=True)
            cen = conv - mu
            var = jnp.mean(cen * cen, axis=-1, keepdims=True)
            yn = cen * lax.rsqrt(var + EPS) * cg_ref[...] + cbeta_ref[...]
            y_buf[rows, conv_cols] = (yn * _sigmoid(yn) * silu_z(rows, shifted_cols(pool_cols, A_W))).astype(BF16)
        needs = ([("conv", c, q) for c in range(len(col_chunks)) for q in range(len(conv_rows))]
                 + [("z", j) for j in range(n_z // 2, n_z)])
        step(vector_steps, ("norm_gate", r), 190, needs, norm_gate)

    def emit(queues):
        done, spent, pos = set(), [0] * len(queues), [0] * len(queues)
        while any(pos[q] < len(queues[q]) for q in range(len(queues))):
            ready = [q for q in range(len(queues)) if pos[q] < len(queues[q])
                     and all(n in done for n in queues[q][pos[q]][2])]
            assert ready, "step ordering deadlock"
            q = min(ready, key=lambda q: spent[q])
            name, cost, _, fn = queues[q][pos[q]]
            fn()
            done.add(name)
            spent[q] += cost
            pos[q] += 1

    for rows in ew_rows:
        h_buf[rows, :] = _rms_norm_rows(x_ref[0, rows, :], ng_ref[...]).astype(BF16)
    hp_buf[...] = _dot(perm_ref[0], h_buf[...]).astype(BF16)
    emit([matmul_steps, vector_steps])


def _pool_conv_layer(x, norm_g, w_in, pool_w, pool_scale, conv_w, conv_b, cn_g, cn_b, w_out):
    B, S, D = x.shape
    ts = SEQ_TILE
    const2 = lambda b, i: (0, 0)
    const3 = lambda b, i: (0, 0, 0)
    single = pl.Buffered(1)
    return pl.pallas_call(
        _pool_conv_kernel,
        grid=(B, S // ts),
        in_specs=[
            pl.BlockSpec((1, ts, D), lambda b, i: (b, i, 0)),
            pl.BlockSpec((2, ts, ts), const3, pipeline_mode=single),
            pl.BlockSpec((1, D), const2),
            pl.BlockSpec((D, IN0_W), const2, pipeline_mode=single),
            pl.BlockSpec((len(POOL_WINDOWS), POOL_CH, POOL_CH), const3, pipeline_mode=single),
            pl.BlockSpec((1, A_W), const2),
            pl.BlockSpec((CONV_K, B_W), const2),
            pl.BlockSpec((1, B_W), const2),
            pl.BlockSpec((1, B_W), const2),
            pl.BlockSpec((1, B_W), const2),
            pl.BlockSpec((MIX_W, D), const2, pipeline_mode=single),
        ],
        out_specs=pl.BlockSpec((1, ts, D), lambda b, i: (b, i, 0)),
        out_shape=jax.ShapeDtypeStruct((B, S, D), F32),
        scratch_shapes=[
            pltpu.VMEM((2 * ts, A_W), F32),
            pltpu.VMEM((2 * ts, B_W), F32),
            pltpu.VMEM((ts, D), BF16),
            pltpu.VMEM((ts, D), BF16),
            pltpu.VMEM((ts, IN0_W - A_W), F32),
            pltpu.VMEM((ts, A_W), BF16),
            pltpu.VMEM((ts, A_W), F32),
            pltpu.VMEM((ts, B_W), F32),
            pltpu.VMEM((ts, MIX_W), BF16),
            pltpu.VMEM((ts, MIX_W), BF16),
        ],
        compiler_params=pltpu.CompilerParams(
            dimension_semantics=("arbitrary", "arbitrary"), vmem_limit_bytes=VMEM_LIMIT),
        name="pool_conv_layer",
    )(x, _strand_permutation(ts), norm_g.reshape(1, D), w_in.astype(BF16), pool_w.astype(BF16),
      pool_scale.reshape(1, A_W),
      conv_w, conv_b.reshape(1, B_W), cn_g.reshape(1, B_W), cn_b.reshape(1, B_W), w_out.astype(BF16))


def _head_rms_norm(t, g_row):
    low = lax.broadcasted_iota(jnp.int32, (1, LANES), 1) < HEAD_DIM
    cols = []
    for c in range(ATT_W // LANES):
        tc = t[:, c * LANES:(c + 1) * LANES]
        sq = tc * tc
        s_low = jnp.sum(jnp.where(low, sq, 0.0), axis=-1, keepdims=True)
        s_high = jnp.sum(jnp.where(low, 0.0, sq), axis=-1, keepdims=True)
        ms = jnp.where(low, s_low, s_high) * (1.0 / HEAD_DIM)
        cols.append(tc * lax.rsqrt(ms + EPS) * g_row[:, c * LANES:(c + 1) * LANES])
    return cols


def _moba_proj_kernel(x_ref, ng_ref, win_ref, qg_ref, kg_ref, q_ref, k_ref, vt_ref, gz_ref, km_ref):
    x = x_ref[0]
    h = _rms_norm_rows(x, ng_ref[...]).astype(BF16)

    q = _dot(h, win_ref[:, 0:ATT_W])
    for c, qc in enumerate(_head_rms_norm(q, qg_ref[...])):
        q_ref[0, :, c * LANES:(c + 1) * LANES] = (qc * (HEAD_DIM ** -0.5 * LOG2_E)).astype(BF16)

    k = _dot(h, win_ref[:, ATT_W:2 * ATT_W])
    for c, kc in enumerate(_head_rms_norm(k, kg_ref[...])):
        k_ref[0, :, c * LANES:(c + 1) * LANES] = kc.astype(BF16)
        km_ref[0, 0, :, c * LANES:(c + 1) * LANES] = jnp.mean(kc, axis=0, keepdims=True)

    v = _dot(h, win_ref[:, 2 * ATT_W:3 * ATT_W])
    vt_ref[0, 0] = v.T.astype(BF16)

    z = _dot(h, win_ref[:, 3 * ATT_W:4 * ATT_W])
    gz_ref[0] = (z * _sigmoid(z)).astype(BF16)


def _moba_proj(x, norm_g, w_in, q_norm_g, k_norm_g):
    B, S, D = x.shape
    ts = MOBA_BLOCK
    nb = S // ts
    const2 = lambda b, i: (0, 0)
    row_blk = pl.BlockSpec((1, ts, ATT_W), lambda b, i: (b, i, 0))
    return pl.pallas_call(
        _moba_proj_kernel,
        grid=(B, nb),
        in_specs=[
            pl.BlockSpec((1, ts, D), lambda b, i: (b, i, 0)),
            pl.BlockSpec((1, D), const2),
            pl.BlockSpec((D, 4 * ATT_W), const2, pipeline_mode=pl.Buffered(1)),
            pl.BlockSpec((1, ATT_W), const2),
            pl.BlockSpec((1, ATT_W), const2),
        ],
        out_specs=[
            row_blk,
            row_blk,
            pl.BlockSpec((1, 1, ATT_W, ts), lambda b, i: (b, i, 0, 0)),
            row_blk,
            pl.BlockSpec((1, 1, 1, ATT_W), lambda b, i: (b, i, 0, 0)),
        ],
        out_shape=[
            jax.ShapeDtypeStruct((B, S, ATT_W), BF16),
            jax.ShapeDtypeStruct((B, S, ATT_W), BF16),
            jax.ShapeDtypeStruct((B, nb, ATT_W, ts), BF16),
            jax.ShapeDtypeStruct((B, S, ATT_W), BF16),
            jax.ShapeDtypeStruct((B, nb, 1, ATT_W), F32),
        ],
        compiler_params=pltpu.CompilerParams(
            dimension_semantics=("arbitrary", "arbitrary"), vmem_limit_bytes=VMEM_LIMIT),
        name="moba_proj",
    )(x, norm_g.reshape(1, D), w_in.astype(BF16),
      jnp.tile(q_norm_g, N_HEADS).reshape(1, ATT_W), jnp.tile(k_norm_g, N_HEADS).reshape(1, ATT_W))


def _moba_attn_kernel(slopes_ref, q_lo_ref, q_hi_ref, k_ref, vt_ref, km_ref, gz_lo_ref, gz_hi_ref,
                      o_lo_ref, o_hi_ref, qa_ref, bias_ref, *head_scratch):
    blk = MOBA_BLOCK
    n_h = ATTN_HEADS_PER_STEP
    n_sets = ATTN_SCRATCH_SETS
    s_refs = [head_scratch[h % n_sets] for h in range(n_h)]
    p_refs = [head_scratch[n_sets + h % n_sets] for h in range(n_h)]
    heads = range(n_h)

    def lanes_of(h):
        g0 = (h // HEADS_PER_GROUP) * LANES
        return slice(g0, g0 + LANES)
    nb = km_ref.shape[1]
    half = nb // 2
    p = pl.program_id(1)
    step = pl.program_id(2)
    neg_inf = jnp.float32(-jnp.inf)
    LO, HI = 0, 1
    q_blk = (step, nb - 1 - step)
    q_refs = (q_lo_ref, q_hi_ref)

    km = km_ref[0]
    row = lax.broadcasted_iota(jnp.int32, (LANES, 1), 0)
    lane = lax.broadcasted_iota(jnp.int32, (1, LANES), 1)
    blk_id = lax.broadcasted_iota(jnp.int32, (nb, blk), 0)
    blk_f = blk_id.astype(F32)
    key_off = lax.broadcasted_iota(jnp.int32, (blk, LANES), 0).astype(F32)
    k_extra = jnp.where(lane < 3, key_off, 0.0).astype(BF16)
    causal = (lax.broadcasted_iota(jnp.int32, (blk, blk), 1)
              >= lax.broadcasted_iota(jnp.int32, (blk, blk), 0))

    def fold8(t, op):
        parts = [t[r:r + SUBLANES] for r in range(0, t.shape[0], SUBLANES)]
        while len(parts) > 1:
            parts = [op(parts[a], parts[a + 1]) for a in range(0, len(parts), 2)]
        return parts[0]

    cslope = []
    for h in heads:
        hh = h % HEADS_PER_GROUP
        cs = slopes_ref[p * n_h + h] * LOG2_E
        cslope.append(cs)
        cs_v = jnp.full((LANES, blk), cs, F32)
        cs_hi = cs_v.astype(BF16).astype(F32)
        cs_mid = (cs_v - cs_hi).astype(BF16).astype(F32)
        cs_lo = cs_v - cs_hi - cs_mid
        q_extra = jnp.where(row == 0, cs_hi, jnp.where(row == 1, cs_mid, jnp.where(row == 2, cs_lo, 0.0)))
        q_extra = q_extra.astype(BF16)
        for qb in (LO, HI):
            q_t = q_refs[qb][0, :, lanes_of(h)].astype(F32).T
            q_head = jnp.where(row // HEAD_DIM == hh, q_t, 0.0).astype(BF16)
            qa_ref[qb, h] = jnp.concatenate([q_head, q_extra], axis=0)

    def choose_blocks(h, qb):
        i = q_blk[qb]
        kmh = jnp.where(lane // HEAD_DIM == h % HEADS_PER_GROUP, km[:, lanes_of(h)], 0.0)
        km_hi = kmh.astype(BF16)
        km_lo = (kmh - km_hi.astype(F32)).astype(BF16)
        q_head = qa_ref[qb, h, 0:LANES, :]
        gate = _dot(km_hi, q_head) + _dot(km_lo, q_head)
        avail = jnp.where(blk_id < i, 1.0, 0.0)
        keep = jnp.zeros((nb, blk), F32)
        for _ in range(MOBA_TOPK):
            g = jnp.where(avail > 0.0, gate, neg_inf)
            top = jnp.max(g, axis=0, keepdims=True)
            first = jnp.min(jnp.where((g == top) & (avail > 0.0), blk_f, float(nb)), axis=0, keepdims=True)
            pick = jnp.where(blk_f == first, 1.0, 0.0)
            keep = keep + pick
            avail = avail - pick
        dist = ((blk_id - i) * blk).astype(F32)
        bias_ref[qb, h] = jnp.where(keep > 0.0, cslope[h] * dist, jnp.where(blk_id == i, 0.0, neg_inf))

    def key_rows(j):
        return pl.ds(pl.multiple_of(j * blk, blk), blk)

    n_slots = nb + 1
    slot_is_lo, slot_key, slot_causal = [], [], []
    for t in range(n_slots):
        if t <= half:
            slot_is_lo.append(None)
            slot_key.append(q_blk[HI] - half + t)
            slot_causal.append(t == half)
        else:
            u = t - (half + 1)
            is_lo = u >= q_blk[HI] - half
            slot_is_lo.append(is_lo)
            slot_key.append(jnp.where(is_lo, u - (q_blk[HI] - half), u))
            slot_causal.append(t == n_slots - 1)

    def pick(is_lo, lo, hi):
        return hi if is_lo is None else jnp.where(is_lo, lo, hi)

    def bias_row(t, hh):
        qb = HI if slot_is_lo[t] is None else jnp.where(slot_is_lo[t], LO, HI)
        return bias_ref[qb, hh, pl.ds(slot_key[t], 1), :]


    m8 = [[jnp.full((SUBLANES, blk), neg_inf, F32) for _ in heads] for _ in (LO, HI)]
    m_row = [[None] * n_h for _ in (LO, HI)]
    acc = [[None] * n_h for _ in (LO, HI)]
    norm = [[None] * n_h for _ in (LO, HI)]
    n_fixed = half + 1

    def pass1(t, hh):
        is_lo = slot_is_lo[t]
        k_aug = jnp.concatenate([k_ref[0, key_rows(slot_key[t]), lanes_of(hh)], k_extra], axis=1)
        qa = qa_ref[HI, hh] if is_lo is None else qa_ref[jnp.where(is_lo, LO, HI), hh]
        s = _dot(k_aug, qa)
        if slot_causal[t]:
            s = jnp.where(causal, s, neg_inf)
        s_refs[hh][t] = s
        top = fold8(s, jnp.maximum) + bias_row(t, hh)
        if is_lo is None:
            m8[HI][hh] = jnp.maximum(m8[HI][hh], top)
        else:
            m8[LO][hh] = jnp.maximum(m8[LO][hh], jnp.where(is_lo, top, neg_inf))
            m8[HI][hh] = jnp.maximum(m8[HI][hh], jnp.where(is_lo, neg_inf, top))

    def row_max(hh):
        for qb in (LO, HI):
            m_row[qb][hh] = jnp.max(m8[qb][hh], axis=0, keepdims=True)

    def pass2(t, hh):
        shift = pick(slot_is_lo[t], m_row[LO][hh], m_row[HI][hh]) - bias_row(t, hh)
        p_refs[hh][t] = jnp.exp2(s_refs[hh][t] - shift).astype(BF16)

    ones_rows = jnp.ones((BF16_ROWS, blk), F32)
    v_rows = HEAD_DIM + BF16_ROWS

    def values_and_ones(t, hh):
        return jnp.concatenate(
            [vt_ref[0, slot_key[t], hh * HEAD_DIM:(hh + 1) * HEAD_DIM, :].astype(F32), ones_rows], axis=0)

    pv_fixed = [None] * n_h

    def value_product_fixed(hh):
        v_fixed = jnp.concatenate([values_and_ones(t, hh).astype(BF16) for t in range(n_fixed)], axis=1)
        pv_fixed[hh] = _dot(v_fixed, p_refs[hh][0:n_fixed].reshape(n_fixed * blk, blk))

    def value_product_split(hh):
        pieces = []
        for t in range(n_fixed, n_slots):
            v32 = values_and_ones(t, hh)
            pieces.append(jnp.concatenate([jnp.where(slot_is_lo[t], v32, 0.0),
                                           jnp.where(slot_is_lo[t], 0.0, v32)], axis=0).astype(BF16))
        v_split = jnp.concatenate(pieces, axis=1)
        pv_split = _dot(v_split, p_refs[hh][n_fixed:n_slots].reshape((n_slots - n_fixed) * blk, blk))
        pv_hi = pv_fixed[hh] + pv_split[v_rows:2 * v_rows]
        acc[LO][hh] = pv_split[0:HEAD_DIM]
        norm[LO][hh] = pv_split[HEAD_DIM:HEAD_DIM + 1]
        acc[HI][hh] = pv_hi[0:HEAD_DIM]
        norm[HI][hh] = pv_hi[HEAD_DIM:HEAD_DIM + 1]

    for h in heads:
        for qb in (LO, HI):
            choose_blocks(h, qb)
    for t in range(n_slots):
        pass1(t, 0)
    row_max(0)
    for h in heads:
        for t in range(n_slots):
            pass2(t, h)
            if h + 1 < n_h:
                pass1(t, h + 1)
            if t == 2 and h > 0:
                value_product_split(h - 1)
            if t == n_fixed:
                value_product_fixed(h)
        if h + 1 < n_h:
            row_max(h + 1)
    value_product_split(n_h - 1)

    for qb, o_ref, gz_ref in ((LO, o_lo_ref, gz_lo_ref), (HI, o_hi_ref, gz_hi_ref)):
        o_t = jnp.concatenate([acc[qb][hh] / norm[qb][hh] for hh in heads], axis=0)
        o_ref[0] = (o_t.T * gz_ref[0].astype(F32)).astype(BF16)


def _moba_attn(q, k, vt, kmean, gz, slopes):
    B, S, _ = q.shape
    blk = MOBA_BLOCK
    nb = S // blk
    n_h = ATTN_HEADS_PER_STEP
    width = n_h * HEAD_DIM
    half = nb // 2
    lo_blk = pl.BlockSpec((1, blk, width), lambda b, p, s: (b, s, p))
    hi_blk = pl.BlockSpec((1, blk, width), lambda b, p, s: (b, nb - 1 - s, p))
    return pl.pallas_call(
        _moba_attn_kernel,
        grid=(B, ATT_W // width, half),
        in_specs=[
            pl.BlockSpec(memory_space=pltpu.SMEM),
            lo_blk,
            hi_blk,
            pl.BlockSpec((1, S, width), lambda b, p, s: (b, 0, p)),
            pl.BlockSpec((1, nb, width, blk), lambda b, p, s: (b, 0, p, 0)),
            pl.BlockSpec((1, nb, width), lambda b, p, s: (b, 0, p)),
            lo_blk,
            hi_blk,
        ],
        out_specs=[
            lo_blk,
            pl.BlockSpec((1, blk, width), lambda b, p, s: (b, half - 1 - s, p)),
        ],
        out_shape=[jax.ShapeDtypeStruct((B, S // 2, ATT_W), BF16)] * 2,
        scratch_shapes=[
            pltpu.VMEM((2, n_h, 2 * LANES, blk), BF16),
            pltpu.VMEM((2, n_h, nb, blk), F32),
        ] + [pltpu.VMEM((nb + 1, blk, blk), F32)] * ATTN_SCRATCH_SETS
          + [pltpu.VMEM((nb + 1, blk, blk), BF16)] * ATTN_SCRATCH_SETS,
        compiler_params=pltpu.CompilerParams(
            dimension_semantics=("arbitrary", "arbitrary", "arbitrary"), vmem_limit_bytes=VMEM_LIMIT),
        name="moba_attn",
    )(slopes, q, q, k, vt, kmean, gz, gz)


def _moba_out_kernel(x_ref, og_lo_ref, og_hi_ref, w_ref, o_ref, *, tiles_per_half):
    in_lo = pl.program_id(1) < tiles_per_half

    @pl.when(in_lo)
    def _():
        o_ref[0] = x_ref[0] + _dot(og_lo_ref[0], w_ref[...])

    @pl.when(jnp.logical_not(in_lo))
    def _():
        o_ref[0] = x_ref[0] + _dot(og_hi_ref[0], w_ref[...])


def _moba_out(x, og_lo, og_hi, w_out):
    B, S, D = x.shape
    ts = 4 * SEQ_TILE
    per_half = S // 2 // ts
    return pl.pallas_call(
        functools.partial(_moba_out_kernel, tiles_per_half=per_half),
        grid=(B, S // ts),
        in_specs=[
            pl.BlockSpec((1, ts, D), lambda b, i: (b, i, 0)),
            pl.BlockSpec((1, ts, ATT_W), lambda b, i: (b, jnp.minimum(i, per_half - 1), 0)),
            pl.BlockSpec((1, ts, ATT_W), lambda b, i: (b, jnp.maximum(i - per_half, 0), 0)),
            pl.BlockSpec((ATT_W, D), lambda b, i: (0, 0), pipeline_mode=pl.Buffered(1)),
        ],
        out_specs=pl.BlockSpec((1, ts, D), lambda b, i: (b, i, 0)),
        out_shape=jax.ShapeDtypeStruct((B, S, D), F32),
        compiler_params=pltpu.CompilerParams(
            dimension_semantics=("arbitrary", "arbitrary"), vmem_limit_bytes=VMEM_LIMIT),
        name="moba_out",
    )(x, og_lo, og_hi, w_out.astype(BF16))


def kernel(x, norm0_g, w_in0, pool_w, pool_scale, conv_w, conv_b, cnorm_g, cnorm_b, w_out0,
           norm1_g, w_in1, q_norm_g, k_norm_g, w_out1):
    assert x.shape[1] % MOBA_BLOCK == 0 and x.shape[2] == D_MODEL
    x = _pool_conv_layer(x, norm0_g[0], w_in0[0], pool_w[0], pool_scale[0], conv_w[0], conv_b[0],
                         cnorm_g[0], cnorm_b[0], w_out0[0])
    q, k, vt, gz, kmean = _moba_proj(x, norm1_g[0], w_in1[0], q_norm_g[0], k_norm_g[0])
    slopes = jnp.exp2(-8.0 * jnp.arange(1, N_HEADS + 1, dtype=F32) / N_HEADS)
    og_lo, og_hi = _moba_attn(q, k, vt, kmean.reshape(kmean.shape[0], -1, ATT_W), gz, slopes)
    return _moba_out(x, og_lo, og_hi, w_out1[0])
```

```python
import functools
import math

import jax
import jax.numpy as jnp
import numpy as np
from jax import lax
from jax.experimental import pallas as pl
from jax.experimental.pallas import tpu as pltpu

D_MODEL = 1024
A_W = 1024
B_W = 1024
MIX_W = 2048
POOL_WINDOWS = (2, 4, 8, 16)
POOL_CH = 256
CONV_K = 31
IN0_W = A_W + 2 * B_W + MIX_W
HEAD_DIM = 64
N_HEADS = 16
ATT_W = 1024
MOBA_BLOCK = 256
MOBA_TOPK = 3
EPS = 1e-6
LOG2_E = math.log2(math.e)

SEQ_TILE = 256
EW_ROWS = 32
CONV_ROWS = 64
CONV_COLS = 256
LANES = 128
SUBLANES = 8
BF16_ROWS = 16
HEADS_PER_GROUP = LANES // HEAD_DIM
ATTN_HEADS_PER_STEP = 8
ATTN_SCRATCH_SETS = 3
SLOPE_TERMS = 3
VMEM_LIMIT = 56 * 1024 * 1024

F32 = jnp.float32
BF16 = jnp.bfloat16


def _dot(a, b):
    return jnp.dot(a, b, preferred_element_type=F32)


def _rms_norm_rows(x, g):
    ms = jnp.mean(x * x, axis=-1, keepdims=True)
    return x * lax.rsqrt(ms + EPS) * g


def _sigmoid(x):
    return 1.0 / (1.0 + jnp.exp(-x))


def _strand_permutation(ts):
    r = np.arange(ts)
    time_of_row = (r % SUBLANES) * (ts // SUBLANES) + r // SUBLANES
    to_perm = (r[None, :] == time_of_row[:, None]).astype(np.float32)
    return jnp.asarray(np.stack([to_perm, to_perm.T]), BF16)


def _pool_conv_kernel(x_ref, perm_ref, ng_ref, win_ref, pw_ref, ps_ref, cw_ref, cb_ref, cg_ref, cbeta_ref,
                      wout_ref, o_ref, a_buf, u_buf, h_buf, hp_buf, proj_buf, pool_buf, mix_buf, conv_buf,
                      y_buf, ynat_buf):
    ts = SEQ_TILE
    strand = ts // SUBLANES
    i = pl.program_id(1)

    row = lax.broadcasted_iota(jnp.int32, (ts, 1), 0)
    first_strand = row % SUBLANES == 0
    tpos = i * ts + (row % SUBLANES) * strand + row // SUBLANES

    @pl.when(i == 0)
    def _():
        a_buf[ts:2 * ts, :] = jnp.zeros((ts, A_W), F32)
        u_buf[ts:2 * ts, :] = jnp.zeros((ts, B_W), F32)

    def last_strand_to_front(buf, cols):
        buf[0:ts, cols] = pltpu.roll(buf[ts:2 * ts, cols], ts - (SUBLANES - 1), 0)

    def finish_shift(buf, cols):
        buf[0:ts, cols] = jnp.where(first_strand, buf[0:ts, cols], pltpu.roll(buf[ts:2 * ts, cols], 1, 0))

    def delayed(buf, d, rows, cols):
        start = (strand - d) * SUBLANES + rows.start
        return buf[start:start + (rows.stop - rows.start), cols]

    def row_chunks(n):
        return [slice(r, r + n) for r in range(0, ts, n)]

    def shifted_cols(cols, by):
        return slice(cols.start + by, cols.stop + by)

    def silu_z(rows, cols):
        z = proj_buf[rows, shifted_cols(cols, 2 * B_W)]
        return z * _sigmoid(z)

    col_chunks = [slice(c, c + CONV_COLS) for c in range(0, B_W, CONV_COLS)]
    conv_rows = row_chunks(CONV_ROWS)
    ew_rows = row_chunks(EW_ROWS)
    matmul_steps, vector_steps = [], []

    def step(queue, name, cost, needs, fn):
        queue.append((name, cost, tuple(needs), fn))

    def in_proj(c0):
        def run():
            proj_buf[:, c0:c0 + CONV_COLS] = _dot(hp_buf[...], win_ref[:, A_W + c0:A_W + c0 + CONV_COLS])
        return run

    n_z = MIX_W // CONV_COLS
    for c, cols in enumerate(col_chunks):
        step(matmul_steps, ("bv", c), 256, [], in_proj(cols.start))
        step(matmul_steps, ("bg", c), 256, [], in_proj(B_W + cols.start))
    for c, cols in enumerate(col_chunks):
        def pool_in(cols=cols):
            last_strand_to_front(a_buf, cols)
            a_buf[ts:2 * ts, cols] = _dot(hp_buf[...], win_ref[:, cols])
            finish_shift(a_buf, cols)
        step(matmul_steps, ("a", c), 256, [], pool_in)
    for j in range(n_z):
        step(matmul_steps, ("z", j), 256, [], in_proj(2 * B_W + j * CONV_COLS))
    for g in range(len(POOL_WINDOWS)):
        def pool_mix(g=g):
            gc = col_chunks[g]
            mix_buf[:, gc] = _dot(pool_buf[:, gc], pw_ref[g])
        step(matmul_steps, ("pool_mix", g), 64, [("pool", g, r) for r in range(len(conv_rows))], pool_mix)

    def unpermute(cols):
        def run():
            ynat_buf[:, cols] = _dot(perm_ref[1], y_buf[:, cols]).astype(BF16)
        return run

    pool_cols, conv_cols = slice(0, A_W), slice(A_W, MIX_W)

    def out_first():
        o_ref[0] = x_ref[0] + _dot(ynat_buf[:, pool_cols], wout_ref[pool_cols, :])

    def out_second():
        o_ref[0] += _dot(ynat_buf[:, conv_cols], wout_ref[conv_cols, :])

    step(matmul_steps, "unperm_pool", 256, [("pool_gate", r) for r in range(len(ew_rows))], unpermute(pool_cols))
    step(matmul_steps, "out_pool", 1024, ["unperm_pool"], out_first)
    step(matmul_steps, "unperm_conv", 256, [("norm_gate", r) for r in range(len(ew_rows))], unpermute(conv_cols))
    step(matmul_steps, "out_conv", 1024, ["unperm_conv", "out_pool"], out_second)

    for c, cols in enumerate(col_chunks):
        def glu(cols=cols):
            last_strand_to_front(u_buf, cols)
            for rows in conv_rows:
                u_buf[ts + rows.start:ts + rows.stop, cols] = (
                    proj_buf[rows, cols] * _sigmoid(proj_buf[rows, shifted_cols(cols, B_W)]))
            finish_shift(u_buf, cols)
        step(vector_steps, ("glu", c), 200, [("bv", c), ("bg", c)], glu)
        for r, rows in enumerate(conv_rows):
            def conv(rows=rows, cols=cols):
                acc = jnp.broadcast_to(cb_ref[:, cols], (CONV_ROWS, CONV_COLS))
                for k in range(CONV_K):
                    acc = acc + cw_ref[k:k + 1, cols] * delayed(u_buf, CONV_K - 1 - k, rows, cols)
                conv_buf[rows, cols] = acc
            step(vector_steps, ("conv", c, r), 250, [("glu", c)], conv)

    for g, w in enumerate(POOL_WINDOWS):
        for r, rows in enumerate(conv_rows):
            def pool(g=g, w=w, rows=rows):
                cols = col_chunks[g]
                cur = delayed(a_buf, 0, rows, cols)
                win = cur
                for d in range(1, w):
                    win = win + delayed(a_buf, d, rows, cols)
                cnt = jnp.minimum(tpos[rows] + 1, w).astype(F32)
                pool_buf[rows, cols] = (win / cnt - cur).astype(BF16)
            step(vector_steps, ("pool", g, r), 30, [("a", g)], pool)
    for r, rows in enumerate(ew_rows):
        def pool_gate(rows=rows):
            y_buf[rows, pool_cols] = (mix_buf[rows, :] * ps_ref[...] * silu_z(rows, pool_cols)).astype(BF16)
        needs = [("pool_mix", g) for g in range(len(POOL_WINDOWS))] + [("z", j) for j in range(n_z // 2)]
        step(vector_steps, ("pool_gate", r), 80, needs, pool_gate)

    for r, rows in enumerate(ew_rows):
        def norm_gate(rows=rows):
            conv = conv_buf[rows, :]
            mu = jnp.mean(conv, axis=-1, keepdims=True)
            cen = conv - mu
            var = jnp.mean(cen * cen, axis=-1, keepdims=True)
            yn = cen * lax.rsqrt(var + EPS) * cg_ref[...] + cbeta_ref[...]
            y_buf[rows, conv_cols] = (yn * _sigmoid(yn) * silu_z(rows, shifted_cols(pool_cols, A_W))).astype(BF16)
        needs = ([("conv", c, q) for c in range(len(col_chunks)) for q in range(len(conv_rows))]
                 + [("z", j) for j in range(n_z // 2, n_z)])
        step(vector_steps, ("norm_gate", r), 190, needs, norm_gate)

    def emit(queues):
        done, spent, pos = set(), [0] * len(queues), [0] * len(queues)
        while any(pos[q] < len(queues[q]) for q in range(len(queues))):
            ready = [q for q in range(len(queues)) if pos[q] < len(queues[q])
                     and all(n in done for n in queues[q][pos[q]][2])]
            assert ready, "step ordering deadlock"
            q = min(ready, key=lambda q: spent[q])
            name, cost, _, fn = queues[q][pos[q]]
            fn()
            done.add(name)
            spent[q] += cost
            pos[q] += 1

    for rows in ew_rows:
        h_buf[rows, :] = _rms_norm_rows(x_ref[0, rows, :], ng_ref[...]).astype(BF16)
    hp_buf[...] = _dot(perm_ref[0], h_buf[...]).astype(BF16)
    emit([matmul_steps, vector_steps])


def _pool_conv_layer(x, norm_g, w_in, pool_w, pool_scale, conv_w, conv_b, cn_g, cn_b, w_out):
    B, S, D = x.shape
    ts = SEQ_TILE
    const2 = lambda b, i: (0, 0)
    const3 = lambda b, i: (0, 0, 0)
    single = pl.Buffered(1)
    return pl.pallas_call(
        _pool_conv_kernel,
        grid=(B, S // ts),
        in_specs=[
            pl.BlockSpec((1, ts, D), lambda b, i: (b, i, 0)),
            pl.BlockSpec((2, ts, ts), const3, pipeline_mode=single),
            pl.BlockSpec((1, D), const2),
            pl.BlockSpec((D, IN0_W), const2, pipeline_mode=single),
            pl.BlockSpec((len(POOL_WINDOWS), POOL_CH, POOL_CH), const3, pipeline_mode=single),
            pl.BlockSpec((1, A_W), const2),
            pl.BlockSpec((CONV_K, B_W), const2),
            pl.BlockSpec((1, B_W), const2),
            pl.BlockSpec((1, B_W), const2),
            pl.BlockSpec((1, B_W), const2),
            pl.BlockSpec((MIX_W, D), const2, pipeline_mode=single),
        ],
        out_specs=pl.BlockSpec((1, ts, D), lambda b, i: (b, i, 0)),
        out_shape=jax.ShapeDtypeStruct((B, S, D), F32),
        scratch_shapes=[
            pltpu.VMEM((2 * ts, A_W), F32),
            pltpu.VMEM((2 * ts, B_W), F32),
            pltpu.VMEM((ts, D), BF16),
            pltpu.VMEM((ts, D), BF16),
            pltpu.VMEM((ts, IN0_W - A_W), F32),
            pltpu.VMEM((ts, A_W), BF16),
            pltpu.VMEM((ts, A_W), F32),
            pltpu.VMEM((ts, B_W), F32),
            pltpu.VMEM((ts, MIX_W), BF16),
            pltpu.VMEM((ts, MIX_W), BF16),
        ],
        compiler_params=pltpu.CompilerParams(
            dimension_semantics=("arbitrary", "arbitrary"), vmem_limit_bytes=VMEM_LIMIT),
        name="pool_conv_layer",
    )(x, _strand_permutation(ts), norm_g.reshape(1, D), w_in.astype(BF16), pool_w.astype(BF16),
      pool_scale.reshape(1, A_W),
      conv_w, conv_b.reshape(1, B_W), cn_g.reshape(1, B_W), cn_b.reshape(1, B_W), w_out.astype(BF16))


def _head_rms_norm(t, g_row):
    low = lax.broadcasted_iota(jnp.int32, (1, LANES), 1) < HEAD_DIM
    cols = []
    for c in range(ATT_W // LANES):
        tc = t[:, c * LANES:(c + 1) * LANES]
        sq = tc * tc
        s_low = jnp.sum(jnp.where(low, sq, 0.0), axis=-1, keepdims=True)
        s_high = jnp.sum(jnp.where(low, 0.0, sq), axis=-1, keepdims=True)
        ms = jnp.where(low, s_low, s_high) * (1.0 / HEAD_DIM)
        cols.append(tc * lax.rsqrt(ms + EPS) * g_row[:, c * LANES:(c + 1) * LANES])
    return cols


def _moba_proj_kernel(x_ref, ng_ref, win_ref, qg_ref, kg_ref, q_ref, k_ref, vt_ref, gz_ref, km_ref):
    x = x_ref[0]
    h = _rms_norm_rows(x, ng_ref[...]).astype(BF16)

    q = _dot(h, win_ref[:, 0:ATT_W])
    for c, qc in enumerate(_head_rms_norm(q, qg_ref[...])):
        q_ref[0, :, c * LANES:(c + 1) * LANES] = (qc * (HEAD_DIM ** -0.5 * LOG2_E)).astype(BF16)

    k = _dot(h, win_ref[:, ATT_W:2 * ATT_W])
    for c, kc in enumerate(_head_rms_norm(k, kg_ref[...])):
        k_ref[0, :, c * LANES:(c + 1) * LANES] = kc.astype(BF16)
        km_ref[0, 0, :, c * LANES:(c + 1) * LANES] = jnp.mean(kc, axis=0, keepdims=True)

    v = _dot(h, win_ref[:, 2 * ATT_W:3 * ATT_W])
    vt_ref[0, 0] = v.T.astype(BF16)

    z = _dot(h, win_ref[:, 3 * ATT_W:4 * ATT_W])
    gz_ref[0] = (z * _sigmoid(z)).astype(BF16)


def _moba_proj(x, norm_g, w_in, q_norm_g, k_norm_g):
    B, S, D = x.shape
    ts = MOBA_BLOCK
    nb = S // ts
    const2 = lambda b, i: (0, 0)
    row_blk = pl.BlockSpec((1, ts, ATT_W), lambda b, i: (b, i, 0))
    return pl.pallas_call(
        _moba_proj_kernel,
        grid=(B, nb),
        in_specs=[
            pl.BlockSpec((1, ts, D), lambda b, i: (b, i, 0)),
            pl.BlockSpec((1, D), const2),
            pl.BlockSpec((D, 4 * ATT_W), const2, pipeline_mode=pl.Buffered(1)),
            pl.BlockSpec((1, ATT_W), const2),
            pl.BlockSpec((1, ATT_W), const2),
        ],
        out_specs=[
            row_blk,
            row_blk,
            pl.BlockSpec((1, 1, ATT_W, ts), lambda b, i: (b, i, 0, 0)),
            row_blk,
            pl.BlockSpec((1, 1, 1, ATT_W), lambda b, i: (b, i, 0, 0)),
        ],
        out_shape=[
            jax.ShapeDtypeStruct((B, S, ATT_W), BF16),
            jax.ShapeDtypeStruct((B, S, ATT_W), BF16),
            jax.ShapeDtypeStruct((B, nb, ATT_W, ts), BF16),
            jax.ShapeDtypeStruct((B, S, ATT_W), BF16),
            jax.ShapeDtypeStruct((B, nb, 1, ATT_W), F32),
        ],
        compiler_params=pltpu.CompilerParams(
            dimension_semantics=("arbitrary", "arbitrary"), vmem_limit_bytes=VMEM_LIMIT),
        name="moba_proj",
    )(x, norm_g.reshape(1, D), w_in.astype(BF16),
      jnp.tile(q_norm_g, N_HEADS).reshape(1, ATT_W), jnp.tile(k_norm_g, N_HEADS).reshape(1, ATT_W))


def _moba_attn_kernel(slopes_ref, q_lo_ref, q_hi_ref, k_ref, vt_ref, km_ref, gz_lo_ref, gz_hi_ref,
                      o_lo_ref, o_hi_ref, qa_ref, bias_ref, *head_scratch):
    blk = MOBA_BLOCK
    n_h = ATTN_HEADS_PER_STEP
    n_sets = ATTN_SCRATCH_SETS
    s_refs = [head_scratch[h % n_sets] for h in range(n_h)]
    p_refs = [head_scratch[n_sets + h % n_sets] for h in range(n_h)]
    heads = range(n_h)

    def lanes_of(h):
        g0 = (h // HEADS_PER_GROUP) * LANES
        return slice(g0, g0 + LANES)
    nb = km_ref.shape[1]
    half = nb // 2
    p = pl.program_id(1)
    step = pl.program_id(2)
    neg_inf = jnp.float32(-jnp.inf)
    LO, HI = 0, 1
    q_blk = (step, nb - 1 - step)
    q_refs = (q_lo_ref, q_hi_ref)

    km = km_ref[0]
    row = lax.broadcasted_iota(jnp.int32, (LANES, 1), 0)
    lane = lax.broadcasted_iota(jnp.int32, (1, LANES), 1)
    blk_id = lax.broadcasted_iota(jnp.int32, (nb, blk), 0)
    blk_f = blk_id.astype(F32)
    key_off = lax.broadcasted_iota(jnp.int32, (blk, LANES), 0).astype(F32)
    k_extra = jnp.where(lane < SLOPE_TERMS, key_off, 0.0).astype(BF16)
    causal = (lax.broadcasted_iota(jnp.int32, (blk, blk), 1)
              >= lax.broadcasted_iota(jnp.int32, (blk, blk), 0))

    def fold8(t, op):
        parts = [t[r:r + SUBLANES] for r in range(0, t.shape[0], SUBLANES)]
        while len(parts) > 1:
            parts = [op(parts[a], parts[a + 1]) for a in range(0, len(parts), 2)]
        return parts[0]

    cslope = []
    for h in heads:
        hh = h % HEADS_PER_GROUP
        cs = slopes_ref[p * n_h + h] * LOG2_E
        cslope.append(cs)
        cs_v = jnp.full((LANES, blk), cs, F32)
        cs_hi = cs_v.astype(BF16).astype(F32)
        cs_mid = (cs_v - cs_hi).astype(BF16).astype(F32)
        cs_lo = cs_v - cs_hi - cs_mid
        q_extra = jnp.where(row == 0, cs_hi, jnp.where(row == 1, cs_mid, jnp.where(row == 2, cs_lo, 0.0)))
        q_extra = q_extra.astype(BF16)
        for qb in (LO, HI):
            q_t = q_refs[qb][0, :, lanes_of(h)].astype(F32).T
            q_head = jnp.where(row // HEAD_DIM == hh, q_t, 0.0).astype(BF16)
            qa_ref[qb, h] = jnp.concatenate([q_head, q_extra], axis=0)

    def choose_blocks(h, qb):
        i = q_blk[qb]
        kmh = jnp.where(lane // HEAD_DIM == h % HEADS_PER_GROUP, km[:, lanes_of(h)], 0.0)
        km_hi = kmh.astype(BF16)
        km_lo = (kmh - km_hi.astype(F32)).astype(BF16)
        q_head = qa_ref[qb, h, 0:LANES, :]
        gate = _dot(km_hi, q_head) + _dot(km_lo, q_head)
        avail = jnp.where(blk_id < i, 1.0, 0.0)
        keep = jnp.zeros((nb, blk), F32)
        for _ in range(MOBA_TOPK):
            g = jnp.where(avail > 0.0, gate, neg_inf)
            top = jnp.max(g, axis=0, keepdims=True)
            first = jnp.min(jnp.where((g == top) & (avail > 0.0), blk_f, float(nb)), axis=0, keepdims=True)
            pick = jnp.where(blk_f == first, 1.0, 0.0)
            keep = keep + pick
            avail = avail - pick
        dist = ((blk_id - i) * blk).astype(F32)
        bias_ref[qb, h] = jnp.where(keep > 0.0, cslope[h] * dist, jnp.where(blk_id == i, 0.0, neg_inf))

    def key_rows(j):
        return pl.ds(pl.multiple_of(j * blk, blk), blk)

    n_slots = nb + 1
    slot_is_lo, slot_key, slot_causal = [], [], []
    for t in range(n_slots):
        if t <= half:
            slot_is_lo.append(None)
            slot_key.append(q_blk[HI] - half + t)
            slot_causal.append(t == half)
        else:
            u = t - (half + 1)
            is_lo = u >= q_blk[HI] - half
            slot_is_lo.append(is_lo)
            slot_key.append(jnp.where(is_lo, u - (q_blk[HI] - half), u))
            slot_causal.append(t == n_slots - 1)

    def pick(is_lo, lo, hi):
        return hi if is_lo is None else jnp.where(is_lo, lo, hi)

    def bias_row(t, hh):
        qb = HI if slot_is_lo[t] is None else jnp.where(slot_is_lo[t], LO, HI)
        return bias_ref[qb, hh, pl.ds(slot_key[t], 1), :]


    m8 = [[jnp.full((SUBLANES, blk), neg_inf, F32) for _ in heads] for _ in (LO, HI)]
    m_row = [[None] * n_h for _ in (LO, HI)]
    acc = [[None] * n_h for _ in (LO, HI)]
    norm = [[None] * n_h for _ in (LO, HI)]
    n_fixed = half + 1

    def pass1(t, hh):
        is_lo = slot_is_lo[t]
        k_aug = jnp.concatenate([k_ref[0, key_rows(slot_key[t]), lanes_of(hh)], k_extra], axis=1)
        qa = qa_ref[HI, hh] if is_lo is None else qa_ref[jnp.where(is_lo, LO, HI), hh]
        s = _dot(k_aug, qa)
        if slot_causal[t]:
            s = jnp.where(causal, s, neg_inf)
        s_refs[hh][t] = s
        top = fold8(s, jnp.maximum) + bias_row(t, hh)
        if is_lo is None:
            m8[HI][hh] = jnp.maximum(m8[HI][hh], top)
        else:
            m8[LO][hh] = jnp.maximum(m8[LO][hh], jnp.where(is_lo, top, neg_inf))
            m8[HI][hh] = jnp.maximum(m8[HI][hh], jnp.where(is_lo, neg_inf, top))

    def row_max(hh):
        for qb in (LO, HI):
            m_row[qb][hh] = jnp.max(m8[qb][hh], axis=0, keepdims=True)

    def pass2(t, hh):
        shift = pick(slot_is_lo[t], m_row[LO][hh], m_row[HI][hh]) - bias_row(t, hh)
        p_refs[hh][t] = jnp.exp2(s_refs[hh][t] - shift).astype(BF16)

    ones_rows = jnp.ones((BF16_ROWS, blk), F32)
    v_rows = HEAD_DIM + BF16_ROWS

    def values_and_ones(t, hh):
        return jnp.concatenate(
            [vt_ref[0, slot_key[t], hh * HEAD_DIM:(hh + 1) * HEAD_DIM, :].astype(F32), ones_rows], axis=0)

    pv_fixed = [None] * n_h

    def value_product_fixed(hh):
        v_fixed = jnp.concatenate([values_and_ones(t, hh).astype(BF16) for t in range(n_fixed)], axis=1)
        pv_fixed[hh] = _dot(v_fixed, p_refs[hh][0:n_fixed].reshape(n_fixed * blk, blk))

    def value_product_split(hh):
        pieces = []
        for t in range(n_fixed, n_slots):
            v32 = values_and_ones(t, hh)
            pieces.append(jnp.concatenate([jnp.where(slot_is_lo[t], v32, 0.0),
                                           jnp.where(slot_is_lo[t], 0.0, v32)], axis=0).astype(BF16))
        v_split = jnp.concatenate(pieces, axis=1)
        pv_split = _dot(v_split, p_refs[hh][n_fixed:n_slots].reshape((n_slots - n_fixed) * blk, blk))
        pv_hi = pv_fixed[hh] + pv_split[v_rows:2 * v_rows]
        acc[LO][hh] = pv_split[0:HEAD_DIM]
        norm[LO][hh] = pv_split[HEAD_DIM:HEAD_DIM + 1]
        acc[HI][hh] = pv_hi[0:HEAD_DIM]
        norm[HI][hh] = pv_hi[HEAD_DIM:HEAD_DIM + 1]

    for h in heads:
        for qb in (LO, HI):
            choose_blocks(h, qb)
    for t in range(n_slots):
        pass1(t, 0)
    row_max(0)
    for h in heads:
        for t in range(n_slots):
            pass2(t, h)
            if h + 1 < n_h:
                pass1(t, h + 1)
            if t == 2 and h > 0:
                value_product_split(h - 1)
            if t == n_fixed:
                value_product_fixed(h)
        if h + 1 < n_h:
            row_max(h + 1)
    value_product_split(n_h - 1)

    for qb, o_ref, gz_ref in ((LO, o_lo_ref, gz_lo_ref), (HI, o_hi_ref, gz_hi_ref)):
        o_t = jnp.concatenate([acc[qb][hh] / norm[qb][hh] for hh in heads], axis=0)
        o_ref[0] = (o_t.T * gz_ref[0].astype(F32)).astype(BF16)


def _moba_attn(q, k, vt, kmean, gz, slopes):
    B, S, _ = q.shape
    blk = MOBA_BLOCK
    nb = S // blk
    n_h = ATTN_HEADS_PER_STEP
    width = n_h * HEAD_DIM
    half = nb // 2
    lo_blk = pl.BlockSpec((1, blk, width), lambda b, p, s: (b, s, p))
    hi_blk = pl.BlockSpec((1, blk, width), lambda b, p, s: (b, nb - 1 - s, p))
    return pl.pallas_call(
        _moba_attn_kernel,
        grid=(B, ATT_W // width, half),
        in_specs=[
            pl.BlockSpec(memory_space=pltpu.SMEM),
            lo_blk,
            hi_blk,
            pl.BlockSpec((1, S, width), lambda b, p, s: (b, 0, p)),
            pl.BlockSpec((1, nb, width, blk), lambda b, p, s: (b, 0, p, 0)),
            pl.BlockSpec((1, nb, width), lambda b, p, s: (b, 0, p)),
            lo_blk,
            hi_blk,
        ],
        out_specs=[
            lo_blk,
            pl.BlockSpec((1, blk, width), lambda b, p, s: (b, half - 1 - s, p)),
        ],
        out_shape=[jax.ShapeDtypeStruct((B, S // 2, ATT_W), BF16)] * 2,
        scratch_shapes=[
            pltpu.VMEM((2, n_h, 2 * LANES, blk), BF16),
            pltpu.VMEM((2, n_h, nb, blk), F32),
        ] + [pltpu.VMEM((nb + 1, blk, blk), F32)] * ATTN_SCRATCH_SETS
          + [pltpu.VMEM((nb + 1, blk, blk), BF16)] * ATTN_SCRATCH_SETS,
        compiler_params=pltpu.CompilerParams(
            dimension_semantics=("arbitrary", "arbitrary", "arbitrary"), vmem_limit_bytes=VMEM_LIMIT),
        name="moba_attn",
    )(slopes, q, q, k, vt, kmean, gz, gz)


def _moba_out_kernel(x_ref, og_lo_ref, og_hi_ref, w_ref, o_ref, *, tiles_per_half):
    in_lo = pl.program_id(1) < tiles_per_half

    @pl.when(in_lo)
    def _():
        o_ref[0] = x_ref[0] + _dot(og_lo_ref[0], w_ref[...])

    @pl.when(jnp.logical_not(in_lo))
    def _():
        o_ref[0] = x_ref[0] + _dot(og_hi_ref[0], w_ref[...])


def _moba_out(x, og_lo, og_hi, w_out):
    B, S, D = x.shape
    ts = 4 * SEQ_TILE
    per_half = S // 2 // ts
    return pl.pallas_call(
        functools.partial(_moba_out_kernel, tiles_per_half=per_half),
        grid=(B, S // ts),
        in_specs=[
            pl.BlockSpec((1, ts, D), lambda b, i: (b, i, 0)),
            pl.BlockSpec((1, ts, ATT_W), lambda b, i: (b, jnp.minimum(i, per_half - 1), 0)),
            pl.BlockSpec((1, ts, ATT_W), lambda b, i: (b, jnp.maximum(i - per_half, 0), 0)),
            pl.BlockSpec((ATT_W, D), lambda b, i: (0, 0), pipeline_mode=pl.Buffered(1)),
        ],
        out_specs=pl.BlockSpec((1, ts, D), lambda b, i: (b, i, 0)),
        out_shape=jax.ShapeDtypeStruct((B, S, D), F32),
        compiler_params=pltpu.CompilerParams(
            dimension_semantics=("arbitrary", "arbitrary"), vmem_limit_bytes=VMEM_LIMIT),
        name="moba_out",
    )(x, og_lo, og_hi, w_out.astype(BF16))


def kernel(x, norm0_g, w_in0, pool_w, pool_scale, conv_w, conv_b, cnorm_g, cnorm_b, w_out0,
           norm1_g, w_in1, q_norm_g, k_norm_g, w_out1):
    assert x.shape[1] % MOBA_BLOCK == 0 and x.shape[2] == D_MODEL
    x = _pool_conv_layer(x, norm0_g[0], w_in0[0], pool_w[0], pool_scale[0], conv_w[0], conv_b[0],
                         cnorm_g[0], cnorm_b[0], w_out0[0])
    q, k, vt, gz, kmean = _moba_proj(x, norm1_g[0], w_in1[0], q_norm_g[0], k_norm_g[0])
    slopes = jnp.exp2(-8.0 * jnp.arange(1, N_HEADS + 1, dtype=F32) / N_HEADS)
    og_lo, og_hi = _moba_attn(q, k, vt, kmean.reshape(kmean.shape[0], -1, ATT_W), gz, slopes)
    return _moba_out(x, og_lo, og_hi, w_out1[0])
```

```python
import functools
import math

import jax
import jax.numpy as jnp
import numpy as np
from jax import lax
from jax.experimental import pallas as pl
from jax.experimental.pallas import tpu as pltpu

D_MODEL = 1024
A_W = 1024
B_W = 1024
MIX_W = 2048
POOL_WINDOWS = (2, 4, 8, 16)
POOL_CH = 256
CONV_K = 31
IN0_W = A_W + 2 * B_W + MIX_W
HEAD_DIM = 64
N_HEADS = 16
ATT_W = 1024
MOBA_BLOCK = 256
MOBA_TOPK = 3
EPS = 1e-6
LOG2_E = math.log2(math.e)

SEQ_TILE = 256
PROJ_BLOCKS = 4
EW_ROWS = 32
CONV_ROWS = 64
CONV_COLS = 256
LANES = 128
SUBLANES = 8
BF16_ROWS = 16
HEADS_PER_GROUP = LANES // HEAD_DIM
ATTN_HEADS_PER_STEP = 8
ATTN_SCRATCH_SETS = 3
SLOPE_TERMS = 3
VMEM_LIMIT = 56 * 1024 * 1024

F32 = jnp.float32
BF16 = jnp.bfloat16


def _dot(a, b):
    return jnp.dot(a, b, preferred_element_type=F32)


def _rms_norm_rows(x, g):
    ms = jnp.mean(x * x, axis=-1, keepdims=True)
    return x * lax.rsqrt(ms + EPS) * g


def _sigmoid(x):
    return 1.0 / (1.0 + jnp.exp(-x))


def _strand_permutation(ts):
    r = np.arange(ts)
    time_of_row = (r % SUBLANES) * (ts // SUBLANES) + r // SUBLANES
    to_perm = (r[None, :] == time_of_row[:, None]).astype(np.float32)
    return jnp.asarray(np.stack([to_perm, to_perm.T]), BF16)


def _pool_conv_kernel(x_ref, perm_ref, ng_ref, win_ref, pw_ref, ps_ref, cw_ref, cb_ref, cg_ref, cbeta_ref,
                      wout_ref, o_ref, a_buf, u_buf, h_buf, hp_buf, proj_buf, pool_buf, mix_buf, conv_buf,
                      y_buf, ynat_buf):
    ts = SEQ_TILE
    strand = ts // SUBLANES
    i = pl.program_id(1)

    row = lax.broadcasted_iota(jnp.int32, (ts, 1), 0)
    first_strand = row % SUBLANES == 0
    tpos = i * ts + (row % SUBLANES) * strand + row // SUBLANES

    @pl.when(i == 0)
    def _():
        a_buf[ts:2 * ts, :] = jnp.zeros((ts, A_W), F32)
        u_buf[ts:2 * ts, :] = jnp.zeros((ts, B_W), F32)

    def last_strand_to_front(buf, cols):
        buf[0:ts, cols] = pltpu.roll(buf[ts:2 * ts, cols], ts - (SUBLANES - 1), 0)

    def finish_shift(buf, cols):
        buf[0:ts, cols] = jnp.where(first_strand, buf[0:ts, cols], pltpu.roll(buf[ts:2 * ts, cols], 1, 0))

    def delayed(buf, d, rows, cols):
        start = (strand - d) * SUBLANES + rows.start
        return buf[start:start + (rows.stop - rows.start), cols]

    def row_chunks(n):
        return [slice(r, r + n) for r in range(0, ts, n)]

    def shifted_cols(cols, by):
        return slice(cols.start + by, cols.stop + by)

    def silu_z(rows, cols):
        z = proj_buf[rows, shifted_cols(cols, 2 * B_W)]
        return z * _sigmoid(z)

    col_chunks = [slice(c, c + CONV_COLS) for c in range(0, B_W, CONV_COLS)]
    conv_rows = row_chunks(CONV_ROWS)
    ew_rows = row_chunks(EW_ROWS)
    matmul_steps, vector_steps = [], []

    def step(queue, name, cost, needs, fn):
        queue.append((name, cost, tuple(needs), fn))

    def in_proj(c0):
        def run():
            proj_buf[:, c0:c0 + CONV_COLS] = _dot(hp_buf[...], win_ref[:, A_W + c0:A_W + c0 + CONV_COLS])
        return run

    n_z = MIX_W // CONV_COLS
    for c, cols in enumerate(col_chunks):
        step(matmul_steps, ("bv", c), 256, [], in_proj(cols.start))
        step(matmul_steps, ("bg", c), 256, [], in_proj(B_W + cols.start))
    for c, cols in enumerate(col_chunks):
        def pool_in(cols=cols):
            last_strand_to_front(a_buf, cols)
            a_buf[ts:2 * ts, cols] = _dot(hp_buf[...], win_ref[:, cols])
            finish_shift(a_buf, cols)
        step(matmul_steps, ("a", c), 256, [], pool_in)
    for j in range(n_z):
        step(matmul_steps, ("z", j), 256, [], in_proj(2 * B_W + j * CONV_COLS))
    for g in range(len(POOL_WINDOWS)):
        def pool_mix(g=g):
            gc = col_chunks[g]
            mix_buf[:, gc] = _dot(pool_buf[:, gc], pw_ref[g])
        step(matmul_steps, ("pool_mix", g), 64, [("pool", g, r) for r in range(len(conv_rows))], pool_mix)

    def unpermute(cols):
        def run():
            ynat_buf[:, cols] = _dot(perm_ref[1], y_buf[:, cols]).astype(BF16)
        return run

    pool_cols, conv_cols = slice(0, A_W), slice(A_W, MIX_W)

    def out_first():
        o_ref[0] = x_ref[0] + _dot(ynat_buf[:, pool_cols], wout_ref[pool_cols, :])

    def out_second():
        o_ref[0] += _dot(ynat_buf[:, conv_cols], wout_ref[conv_cols, :])

    step(matmul_steps, "unperm_pool", 256, [("pool_gate", r) for r in range(len(ew_rows))], unpermute(pool_cols))
    step(matmul_steps, "out_pool", 1024, ["unperm_pool"], out_first)
    step(matmul_steps, "unperm_conv", 256, [("norm_gate", r) for r in range(len(ew_rows))], unpermute(conv_cols))
    step(matmul_steps, "out_conv", 1024, ["unperm_conv", "out_pool"], out_second)

    for c, cols in enumerate(col_chunks):
        def glu(cols=cols):
            last_strand_to_front(u_buf, cols)
            for rows in conv_rows:
                u_buf[ts + rows.start:ts + rows.stop, cols] = (
                    proj_buf[rows, cols] * _sigmoid(proj_buf[rows, shifted_cols(cols, B_W)]))
            finish_shift(u_buf, cols)
        step(vector_steps, ("glu", c), 200, [("bv", c), ("bg", c)], glu)
        for r, rows in enumerate(conv_rows):
            def conv(rows=rows, cols=cols):
                acc = jnp.broadcast_to(cb_ref[:, cols], (CONV_ROWS, CONV_COLS))
                for k in range(CONV_K):
                    acc = acc + cw_ref[k:k + 1, cols] * delayed(u_buf, CONV_K - 1 - k, rows, cols)
                conv_buf[rows, cols] = acc
            step(vector_steps, ("conv", c, r), 250, [("glu", c)], conv)

    for g, w in enumerate(POOL_WINDOWS):
        for r, rows in enumerate(conv_rows):
            def pool(g=g, w=w, rows=rows):
                cols = col_chunks[g]
                cur = delayed(a_buf, 0, rows, cols)
                win = cur
                for d in range(1, w):
                    win = win + delayed(a_buf, d, rows, cols)
                cnt = jnp.minimum(tpos[rows] + 1, w).astype(F32)
                pool_buf[rows, cols] = (win / cnt - cur).astype(BF16)
            step(vector_steps, ("pool", g, r), 30, [("a", g)], pool)
    for r, rows in enumerate(ew_rows):
        def pool_gate(rows=rows):
            y_buf[rows, pool_cols] = (mix_buf[rows, :] * ps_ref[...] * silu_z(rows, pool_cols)).astype(BF16)
        needs = [("pool_mix", g) for g in range(len(POOL_WINDOWS))] + [("z", j) for j in range(n_z // 2)]
        step(vector_steps, ("pool_gate", r), 80, needs, pool_gate)

    for r, rows in enumerate(ew_rows):
        def norm_gate(rows=rows):
            conv = conv_buf[rows, :]
            mu = jnp.mean(conv, axis=-1, keepdims=True)
            cen = conv - mu
            var = jnp.mean(cen * cen, axis=-1, keepdims=True)
            yn = cen * lax.rsqrt(var + EPS) * cg_ref[...] + cbeta_ref[...]
            y_buf[rows, conv_cols] = (yn * _sigmoid(yn) * silu_z(rows, shifted_cols(pool_cols, A_W))).astype(BF16)
        needs = ([("conv", c, q) for c in range(len(col_chunks)) for q in range(len(conv_rows))]
                 + [("z", j) for j in range(n_z // 2, n_z)])
        step(vector_steps, ("norm_gate", r), 190, needs, norm_gate)

    def emit(queues):
        done, spent, pos = set(), [0] * len(queues), [0] * len(queues)
        while any(pos[q] < len(queues[q]) for q in range(len(queues))):
            ready = [q for q in range(len(queues)) if pos[q] < len(queues[q])
                     and all(n in done for n in queues[q][pos[q]][2])]
            assert ready, "step ordering deadlock"
            q = min(ready, key=lambda q: spent[q])
            name, cost, _, fn = queues[q][pos[q]]
            fn()
            done.add(name)
            spent[q] += cost
            pos[q] += 1

    for rows in ew_rows:
        h_buf[rows, :] = _rms_norm_rows(x_ref[0, rows, :], ng_ref[...]).astype(BF16)
    hp_buf[...] = _dot(perm_ref[0], h_buf[...]).astype(BF16)
    emit([matmul_steps, vector_steps])


def _pool_conv_layer(x, norm_g, w_in, pool_w, pool_scale, conv_w, conv_b, cn_g, cn_b, w_out):
    B, S, D = x.shape
    ts = SEQ_TILE
    const2 = lambda b, i: (0, 0)
    const3 = lambda b, i: (0, 0, 0)
    single = pl.Buffered(1)
    return pl.pallas_call(
        _pool_conv_kernel,
        grid=(B, S // ts),
        in_specs=[
            pl.BlockSpec((1, ts, D), lambda b, i: (b, i, 0)),
            pl.BlockSpec((2, ts, ts), const3, pipeline_mode=single),
            pl.BlockSpec((1, D), const2),
            pl.BlockSpec((D, IN0_W), const2, pipeline_mode=single),
            pl.BlockSpec((len(POOL_WINDOWS), POOL_CH, POOL_CH), const3, pipeline_mode=single),
            pl.BlockSpec((1, A_W), const2),
            pl.BlockSpec((CONV_K, B_W), const2),
            pl.BlockSpec((1, B_W), const2),
            pl.BlockSpec((1, B_W), const2),
            pl.BlockSpec((1, B_W), const2),
            pl.BlockSpec((MIX_W, D), const2, pipeline_mode=single),
        ],
        out_specs=pl.BlockSpec((1, ts, D), lambda b, i: (b, i, 0)),
        out_shape=jax.ShapeDtypeStruct((B, S, D), F32),
        scratch_shapes=[
            pltpu.VMEM((2 * ts, A_W), F32),
            pltpu.VMEM((2 * ts, B_W), F32),
            pltpu.VMEM((ts, D), BF16),
            pltpu.VMEM((ts, D), BF16),
            pltpu.VMEM((ts, IN0_W - A_W), F32),
            pltpu.VMEM((ts, A_W), BF16),
            pltpu.VMEM((ts, A_W), F32),
            pltpu.VMEM((ts, B_W), F32),
            pltpu.VMEM((ts, MIX_W), BF16),
            pltpu.VMEM((ts, MIX_W), BF16),
        ],
        compiler_params=pltpu.CompilerParams(
            dimension_semantics=("arbitrary", "arbitrary"), vmem_limit_bytes=VMEM_LIMIT),
        name="pool_conv_layer",
    )(x, _strand_permutation(ts), norm_g.reshape(1, D), w_in.astype(BF16), pool_w.astype(BF16),
      pool_scale.reshape(1, A_W),
      conv_w, conv_b.reshape(1, B_W), cn_g.reshape(1, B_W), cn_b.reshape(1, B_W), w_out.astype(BF16))


def _head_rms_norm(t, g_row):
    low = lax.broadcasted_iota(jnp.int32, (1, LANES), 1) < HEAD_DIM
    cols = []
    for c in range(ATT_W // LANES):
        tc = t[:, c * LANES:(c + 1) * LANES]
        sq = tc * tc
        s_low = jnp.sum(jnp.where(low, sq, 0.0), axis=-1, keepdims=True)
        s_high = jnp.sum(jnp.where(low, 0.0, sq), axis=-1, keepdims=True)
        ms = jnp.where(low, s_low, s_high) * (1.0 / HEAD_DIM)
        cols.append(tc * lax.rsqrt(ms + EPS) * g_row[:, c * LANES:(c + 1) * LANES])
    return cols


def _moba_proj_kernel(x_ref, ng_ref, win_ref, qg_ref, kg_ref, q_ref, k_ref, vt_ref, gz_ref, km_ref):
    for n in range(PROJ_BLOCKS):
        rows = slice(n * MOBA_BLOCK, (n + 1) * MOBA_BLOCK)
        h = _rms_norm_rows(x_ref[0, rows, :], ng_ref[...]).astype(BF16)

        q = _dot(h, win_ref[:, 0:ATT_W])
        for c, qc in enumerate(_head_rms_norm(q, qg_ref[...])):
            q_ref[0, rows, c * LANES:(c + 1) * LANES] = (qc * (HEAD_DIM ** -0.5 * LOG2_E)).astype(BF16)

        k = _dot(h, win_ref[:, ATT_W:2 * ATT_W])
        for c, kc in enumerate(_head_rms_norm(k, kg_ref[...])):
            k_ref[0, rows, c * LANES:(c + 1) * LANES] = kc.astype(BF16)
            km_ref[0, n, :, c * LANES:(c + 1) * LANES] = jnp.mean(kc, axis=0, keepdims=True)

        v = _dot(h, win_ref[:, 2 * ATT_W:3 * ATT_W])
        vt_ref[0, n] = v.T.astype(BF16)

        z = _dot(h, win_ref[:, 3 * ATT_W:4 * ATT_W])
        gz_ref[0, rows, :] = (z * _sigmoid(z)).astype(BF16)


def _moba_proj(x, norm_g, w_in, q_norm_g, k_norm_g):
    B, S, D = x.shape
    blk = MOBA_BLOCK
    nb = S // blk
    ts = PROJ_BLOCKS * blk
    const2 = lambda b, i: (0, 0)
    row_blk = pl.BlockSpec((1, ts, ATT_W), lambda b, i: (b, i, 0))
    return pl.pallas_call(
        _moba_proj_kernel,
        grid=(B, S // ts),
        in_specs=[
            pl.BlockSpec((1, ts, D), lambda b, i: (b, i, 0)),
            pl.BlockSpec((1, D), const2),
            pl.BlockSpec((D, 4 * ATT_W), const2, pipeline_mode=pl.Buffered(1)),
            pl.BlockSpec((1, ATT_W), const2),
            pl.BlockSpec((1, ATT_W), const2),
        ],
        out_specs=[
            row_blk,
            row_blk,
            pl.BlockSpec((1, PROJ_BLOCKS, ATT_W, blk), lambda b, i: (b, i, 0, 0)),
            row_blk,
            pl.BlockSpec((1, PROJ_BLOCKS, 1, ATT_W), lambda b, i: (b, i, 0, 0)),
        ],
        out_shape=[
            jax.ShapeDtypeStruct((B, S, ATT_W), BF16),
            jax.ShapeDtypeStruct((B, S, ATT_W), BF16),
            jax.ShapeDtypeStruct((B, nb, ATT_W, blk), BF16),
            jax.ShapeDtypeStruct((B, S, ATT_W), BF16),
            jax.ShapeDtypeStruct((B, nb, 1, ATT_W), F32),
        ],
        compiler_params=pltpu.CompilerParams(
            dimension_semantics=("arbitrary", "arbitrary"), vmem_limit_bytes=VMEM_LIMIT),
        name="moba_proj",
    )(x, norm_g.reshape(1, D), w_in.astype(BF16),
      jnp.tile(q_norm_g, N_HEADS).reshape(1, ATT_W), jnp.tile(k_norm_g, N_HEADS).reshape(1, ATT_W))


def _moba_attn_kernel(slopes_ref, q_lo_ref, q_hi_ref, k_ref, vt_ref, km_ref, gz_lo_ref, gz_hi_ref,
                      o_lo_ref, o_hi_ref, qa_ref, bias_ref, *head_scratch):
    blk = MOBA_BLOCK
    n_h = ATTN_HEADS_PER_STEP
    n_sets = ATTN_SCRATCH_SETS
    s_refs = [head_scratch[h % n_sets] for h in range(n_h)]
    p_refs = [head_scratch[n_sets + h % n_sets] for h in range(n_h)]
    heads = range(n_h)

    def lanes_of(h):
        g0 = (h // HEADS_PER_GROUP) * LANES
        return slice(g0, g0 + LANES)
    nb = km_ref.shape[1]
    half = nb // 2
    p = pl.program_id(1)
    step = pl.program_id(2)
    neg_inf = jnp.float32(-jnp.inf)
    LO, HI = 0, 1
    q_blk = (step, nb - 1 - step)
    q_refs = (q_lo_ref, q_hi_ref)

    km = km_ref[0]
    row = lax.broadcasted_iota(jnp.int32, (LANES, 1), 0)
    lane = lax.broadcasted_iota(jnp.int32, (1, LANES), 1)
    blk_id = lax.broadcasted_iota(jnp.int32, (nb, blk), 0)
    blk_f = blk_id.astype(F32)
    key_off = lax.broadcasted_iota(jnp.int32, (blk, LANES), 0).astype(F32)
    k_extra = jnp.where(lane < SLOPE_TERMS, key_off, 0.0).astype(BF16)
    causal = (lax.broadcasted_iota(jnp.int32, (blk, blk), 1)
              >= lax.broadcasted_iota(jnp.int32, (blk, blk), 0))

    def fold8(t, op):
        parts = [t[r:r + SUBLANES] for r in range(0, t.shape[0], SUBLANES)]
        while len(parts) > 1:
            parts = [op(parts[a], parts[a + 1]) for a in range(0, len(parts), 2)]
        return parts[0]

    cslope = []
    for h in heads:
        hh = h % HEADS_PER_GROUP
        cs = slopes_ref[p * n_h + h] * LOG2_E
        cslope.append(cs)
        cs_v = jnp.full((LANES, blk), cs, F32)
        cs_hi = cs_v.astype(BF16).astype(F32)
        cs_mid = (cs_v - cs_hi).astype(BF16).astype(F32)
        cs_lo = cs_v - cs_hi - cs_mid
        q_extra = jnp.where(row == 0, cs_hi, jnp.where(row == 1, cs_mid, jnp.where(row == 2, cs_lo, 0.0)))
        q_extra = q_extra.astype(BF16)
        for qb in (LO, HI):
            q_t = q_refs[qb][0, :, lanes_of(h)].astype(F32).T
            q_head = jnp.where(row // HEAD_DIM == hh, q_t, 0.0).astype(BF16)
            qa_ref[qb, h] = jnp.concatenate([q_head, q_extra], axis=0)

    def choose_blocks(h, qb):
        i = q_blk[qb]
        kmh = jnp.where(lane // HEAD_DIM == h % HEADS_PER_GROUP, km[:, lanes_of(h)], 0.0)
        km_hi = kmh.astype(BF16)
        km_lo = (kmh - km_hi.astype(F32)).astype(BF16)
        q_head = qa_ref[qb, h, 0:LANES, :]
        gate = _dot(km_hi, q_head) + _dot(km_lo, q_head)
        avail = jnp.where(blk_id < i, 1.0, 0.0)
        keep = jnp.zeros((nb, blk), F32)
        for _ in range(MOBA_TOPK):
            g = jnp.where(avail > 0.0, gate, neg_inf)
            top = jnp.max(g, axis=0, keepdims=True)
            first = jnp.min(jnp.where((g == top) & (avail > 0.0), blk_f, float(nb)), axis=0, keepdims=True)
            pick = jnp.where(blk_f == first, 1.0, 0.0)
            keep = keep + pick
            avail = avail - pick
        dist = ((blk_id - i) * blk).astype(F32)
        bias_ref[qb, h] = jnp.where(keep > 0.0, cslope[h] * dist, jnp.where(blk_id == i, 0.0, neg_inf))

    def key_rows(j):
        return pl.ds(pl.multiple_of(j * blk, blk), blk)

    n_slots = nb + 1
    slot_is_lo, slot_key, slot_causal = [], [], []
    for t in range(n_slots):
        if t <= half:
            slot_is_lo.append(None)
            slot_key.append(q_blk[HI] - half + t)
            slot_causal.append(t == half)
        else:
            u = t - (half + 1)
            is_lo = u >= q_blk[HI] - half
            slot_is_lo.append(is_lo)
            slot_key.append(jnp.where(is_lo, u - (q_blk[HI] - half), u))
            slot_causal.append(t == n_slots - 1)

    def pick(is_lo, lo, hi):
        return hi if is_lo is None else jnp.where(is_lo, lo, hi)

    def bias_row(t, hh):
        qb = HI if slot_is_lo[t] is None else jnp.where(slot_is_lo[t], LO, HI)
        return bias_ref[qb, hh, pl.ds(slot_key[t], 1), :]


    m8 = [[jnp.full((SUBLANES, blk), neg_inf, F32) for _ in heads] for _ in (LO, HI)]
    m_row = [[None] * n_h for _ in (LO, HI)]
    acc = [[None] * n_h for _ in (LO, HI)]
    norm = [[None] * n_h for _ in (LO, HI)]
    n_fixed = half + 1

    def pass1(t, hh):
        is_lo = slot_is_lo[t]
        k_aug = jnp.concatenate([k_ref[0, key_rows(slot_key[t]), lanes_of(hh)], k_extra], axis=1)
        qa = qa_ref[HI, hh] if is_lo is None else qa_ref[jnp.where(is_lo, LO, HI), hh]
        s = _dot(k_aug, qa)
        if slot_causal[t]:
            s = jnp.where(causal, s, neg_inf)
        s_refs[hh][t] = s
        top = fold8(s, jnp.maximum) + bias_row(t, hh)
        if is_lo is None:
            m8[HI][hh] = jnp.maximum(m8[HI][hh], top)
        else:
            m8[LO][hh] = jnp.maximum(m8[LO][hh], jnp.where(is_lo, top, neg_inf))
            m8[HI][hh] = jnp.maximum(m8[HI][hh], jnp.where(is_lo, neg_inf, top))

    def row_max(hh):
        for qb in (LO, HI):
            m_row[qb][hh] = jnp.max(m8[qb][hh], axis=0, keepdims=True)

    def pass2(t, hh):
        shift = pick(slot_is_lo[t], m_row[LO][hh], m_row[HI][hh]) - bias_row(t, hh)
        p_refs[hh][t] = jnp.exp2(s_refs[hh][t] - shift).astype(BF16)

    ones_rows = jnp.ones((BF16_ROWS, blk), F32)
    v_rows = HEAD_DIM + BF16_ROWS

    def values_and_ones(t, hh):
        return jnp.concatenate(
            [vt_ref[0, slot_key[t], hh * HEAD_DIM:(hh + 1) * HEAD_DIM, :].astype(F32), ones_rows], axis=0)

    pv_fixed = [None] * n_h

    def value_product_fixed(hh):
        v_fixed = jnp.concatenate([values_and_ones(t, hh).astype(BF16) for t in range(n_fixed)], axis=1)
        pv_fixed[hh] = _dot(v_fixed, p_refs[hh][0:n_fixed].reshape(n_fixed * blk, blk))

    def value_product_split(hh):
        pieces = []
        for t in range(n_fixed, n_slots):
            v32 = values_and_ones(t, hh)
            pieces.append(jnp.concatenate([jnp.where(slot_is_lo[t], v32, 0.0),
                                           jnp.where(slot_is_lo[t], 0.0, v32)], axis=0).astype(BF16))
        v_split = jnp.concatenate(pieces, axis=1)
        pv_split = _dot(v_split, p_refs[hh][n_fixed:n_slots].reshape((n_slots - n_fixed) * blk, blk))
        pv_hi = pv_fixed[hh] + pv_split[v_rows:2 * v_rows]
        acc[LO][hh] = pv_split[0:HEAD_DIM]
        norm[LO][hh] = pv_split[HEAD_DIM:HEAD_DIM + 1]
        acc[HI][hh] = pv_hi[0:HEAD_DIM]
        norm[HI][hh] = pv_hi[HEAD_DIM:HEAD_DIM + 1]

    for h in heads:
        for qb in (LO, HI):
            choose_blocks(h, qb)
    for t in range(n_slots):
        pass1(t, 0)
    row_max(0)
    for h in heads:
        for t in range(n_slots):
            pass2(t, h)
            if h + 1 < n_h:
                pass1(t, h + 1)
            if t == 2 and h > 0:
                value_product_split(h - 1)
            if t == n_fixed:
                value_product_fixed(h)
        if h + 1 < n_h:
            row_max(h + 1)
    value_product_split(n_h - 1)

    for qb, o_ref, gz_ref in ((LO, o_lo_ref, gz_lo_ref), (HI, o_hi_ref, gz_hi_ref)):
        o_t = jnp.concatenate([acc[qb][hh] / norm[qb][hh] for hh in heads], axis=0)
        o_ref[0] = (o_t.T * gz_ref[0].astype(F32)).astype(BF16)


def _moba_attn(q, k, vt, kmean, gz, slopes):
    B, S, _ = q.shape
    blk = MOBA_BLOCK
    nb = S // blk
    n_h = ATTN_HEADS_PER_STEP
    width = n_h * HEAD_DIM
    half = nb // 2
    lo_blk = pl.BlockSpec((1, blk, width), lambda b, p, s: (b, s, p))
    hi_blk = pl.BlockSpec((1, blk, width), lambda b, p, s: (b, nb - 1 - s, p))
    return pl.pallas_call(
        _moba_attn_kernel,
        grid=(B, ATT_W // width, half),
        in_specs=[
            pl.BlockSpec(memory_space=pltpu.SMEM),
            lo_blk,
            hi_blk,
            pl.BlockSpec((1, S, width), lambda b, p, s: (b, 0, p)),
            pl.BlockSpec((1, nb, width, blk), lambda b, p, s: (b, 0, p, 0)),
            pl.BlockSpec((1, nb, width), lambda b, p, s: (b, 0, p)),
            lo_blk,
            hi_blk,
        ],
        out_specs=[
            lo_blk,
            pl.BlockSpec((1, blk, width), lambda b, p, s: (b, half - 1 - s, p)),
        ],
        out_shape=[jax.ShapeDtypeStruct((B, S // 2, ATT_W), BF16)] * 2,
        scratch_shapes=[
            pltpu.VMEM((2, n_h, 2 * LANES, blk), BF16),
            pltpu.VMEM((2, n_h, nb, blk), F32),
        ] + [pltpu.VMEM((nb + 1, blk, blk), F32)] * ATTN_SCRATCH_SETS
          + [pltpu.VMEM((nb + 1, blk, blk), BF16)] * ATTN_SCRATCH_SETS,
        compiler_params=pltpu.CompilerParams(
            dimension_semantics=("arbitrary", "arbitrary", "arbitrary"), vmem_limit_bytes=VMEM_LIMIT),
        name="moba_attn",
    )(slopes, q, q, k, vt, kmean, gz, gz)


def _moba_out_kernel(x_ref, og_lo_ref, og_hi_ref, w_ref, o_ref, *, tiles_per_half):
    in_lo = pl.program_id(1) < tiles_per_half

    @pl.when(in_lo)
    def _():
        o_ref[0] = x_ref[0] + _dot(og_lo_ref[0], w_ref[...])

    @pl.when(jnp.logical_not(in_lo))
    def _():
        o_ref[0] = x_ref[0] + _dot(og_hi_ref[0], w_ref[...])


def _moba_out(x, og_lo, og_hi, w_out):
    B, S, D = x.shape
    ts = 4 * SEQ_TILE
    per_half = S // 2 // ts
    return pl.pallas_call(
        functools.partial(_moba_out_kernel, tiles_per_half=per_half),
        grid=(B, S // ts),
        in_specs=[
            pl.BlockSpec((1, ts, D), lambda b, i: (b, i, 0)),
            pl.BlockSpec((1, ts, ATT_W), lambda b, i: (b, jnp.minimum(i, per_half - 1), 0)),
            pl.BlockSpec((1, ts, ATT_W), lambda b, i: (b, jnp.maximum(i - per_half, 0), 0)),
            pl.BlockSpec((ATT_W, D), lambda b, i: (0, 0), pipeline_mode=pl.Buffered(1)),
        ],
        out_specs=pl.BlockSpec((1, ts, D), lambda b, i: (b, i, 0)),
        out_shape=jax.ShapeDtypeStruct((B, S, D), F32),
        compiler_params=pltpu.CompilerParams(
            dimension_semantics=("arbitrary", "arbitrary"), vmem_limit_bytes=VMEM_LIMIT),
        name="moba_out",
    )(x, og_lo, og_hi, w_out.astype(BF16))


def kernel(x, norm0_g, w_in0, pool_w, pool_scale, conv_w, conv_b, cnorm_g, cnorm_b, w_out0,
           norm1_g, w_in1, q_norm_g, k_norm_g, w_out1):
    assert x.shape[1] % MOBA_BLOCK == 0 and x.shape[2] == D_MODEL
    x = _pool_conv_layer(x, norm0_g[0], w_in0[0], pool_w[0], pool_scale[0], conv_w[0], conv_b[0],
                         cnorm_g[0], cnorm_b[0], w_out0[0])
    q, k, vt, gz, kmean = _moba_proj(x, norm1_g[0], w_in1[0], q_norm_g[0], k_norm_g[0])
    slopes = jnp.exp2(-8.0 * jnp.arange(1, N_HEADS + 1, dtype=F32) / N_HEADS)
    og_lo, og_hi = _moba_attn(q, k, vt, kmean.reshape(kmean.shape[0], -1, ATT_W), gz, slopes)
    return _moba_out(x, og_lo, og_hi, w_out1[0])
```

```python
import functools
import math

import jax
import jax.numpy as jnp
import numpy as np
from jax import lax
from jax.experimental import pallas as pl
from jax.experimental.pallas import tpu as pltpu

D_MODEL = 1024
A_W = 1024
B_W = 1024
MIX_W = 2048
POOL_WINDOWS = (2, 4, 8, 16)
POOL_CH = 256
CONV_K = 31
IN0_W = A_W + 2 * B_W + MIX_W
HEAD_DIM = 64
N_HEADS = 16
ATT_W = 1024
MOBA_BLOCK = 256
MOBA_TOPK = 3
EPS = 1e-6
LOG2_E = math.log2(math.e)

SEQ_TILE = 256
OUT_RING = 3
PROJ_BLOCKS = 4
EW_ROWS = 32
CONV_ROWS = 64
CONV_COLS = 256
LANES = 128
SUBLANES = 8
BF16_ROWS = 16
HEADS_PER_GROUP = LANES // HEAD_DIM
ATTN_HEADS_PER_STEP = 8
ATTN_SCRATCH_SETS = 3
SLOPE_TERMS = 3
VMEM_LIMIT = 56 * 1024 * 1024

F32 = jnp.float32
BF16 = jnp.bfloat16


def _dot(a, b):
    return jnp.dot(a, b, preferred_element_type=F32)


def _rms_norm_rows(x, g):
    ms = jnp.mean(x * x, axis=-1, keepdims=True)
    return x * lax.rsqrt(ms + EPS) * g


def _sigmoid(x):
    return 1.0 / (1.0 + jnp.exp(-x))


def _strand_permutation(ts):
    r = np.arange(ts)
    time_of_row = (r % SUBLANES) * (ts // SUBLANES) + r // SUBLANES
    to_perm = (r[None, :] == time_of_row[:, None]).astype(np.float32)
    return jnp.asarray(np.stack([to_perm, to_perm.T]), BF16)


def _pool_conv_kernel(x_ref, perm_ref, ng_ref, win_ref, pw_ref, ps_ref, cw_ref, cb_ref, cg_ref, cbeta_ref,
                      wout_ref, o_ref, a_buf, u_buf, h_buf, hp_buf, proj_buf, pool_buf, mix_buf, conv_buf,
                      y_buf, ynat_buf):
    ts = SEQ_TILE
    strand = ts // SUBLANES
    i = pl.program_id(1)

    row = lax.broadcasted_iota(jnp.int32, (ts, 1), 0)
    first_strand = row % SUBLANES == 0
    tpos = i * ts + (row % SUBLANES) * strand + row // SUBLANES

    @pl.when(i == 0)
    def _():
        a_buf[ts:2 * ts, :] = jnp.zeros((ts, A_W), F32)
        u_buf[ts:2 * ts, :] = jnp.zeros((ts, B_W), F32)

    def last_strand_to_front(buf, cols):
        buf[0:ts, cols] = pltpu.roll(buf[ts:2 * ts, cols], ts - (SUBLANES - 1), 0)

    def finish_shift(buf, cols):
        buf[0:ts, cols] = jnp.where(first_strand, buf[0:ts, cols], pltpu.roll(buf[ts:2 * ts, cols], 1, 0))

    def delayed(buf, d, rows, cols):
        start = (strand - d) * SUBLANES + rows.start
        return buf[start:start + (rows.stop - rows.start), cols]

    def row_chunks(n):
        return [slice(r, r + n) for r in range(0, ts, n)]

    def shifted_cols(cols, by):
        return slice(cols.start + by, cols.stop + by)

    def silu_z(rows, cols):
        z = proj_buf[rows, shifted_cols(cols, 2 * B_W)]
        return z * _sigmoid(z)

    col_chunks = [slice(c, c + CONV_COLS) for c in range(0, B_W, CONV_COLS)]
    conv_rows = row_chunks(CONV_ROWS)
    ew_rows = row_chunks(EW_ROWS)
    matmul_steps, vector_steps = [], []

    def step(queue, name, cost, needs, fn):
        queue.append((name, cost, tuple(needs), fn))

    def in_proj(c0):
        def run():
            proj_buf[:, c0:c0 + CONV_COLS] = _dot(hp_buf[...], win_ref[:, A_W + c0:A_W + c0 + CONV_COLS])
        return run

    n_z = MIX_W // CONV_COLS
    for c, cols in enumerate(col_chunks):
        step(matmul_steps, ("bv", c), 256, [], in_proj(cols.start))
        step(matmul_steps, ("bg", c), 256, [], in_proj(B_W + cols.start))
    for c, cols in enumerate(col_chunks):
        def pool_in(cols=cols):
            last_strand_to_front(a_buf, cols)
            a_buf[ts:2 * ts, cols] = _dot(hp_buf[...], win_ref[:, cols])
            finish_shift(a_buf, cols)
        step(matmul_steps, ("a", c), 256, [], pool_in)
    for j in range(n_z):
        step(matmul_steps, ("z", j), 256, [], in_proj(2 * B_W + j * CONV_COLS))
    for g in range(len(POOL_WINDOWS)):
        def pool_mix(g=g):
            gc = col_chunks[g]
            mix_buf[:, gc] = _dot(pool_buf[:, gc], pw_ref[g])
        step(matmul_steps, ("pool_mix", g), 64, [("pool", g, r) for r in range(len(conv_rows))], pool_mix)

    def unpermute(cols):
        def run():
            ynat_buf[:, cols] = _dot(perm_ref[1], y_buf[:, cols]).astype(BF16)
        return run

    pool_cols, conv_cols = slice(0, A_W), slice(A_W, MIX_W)

    def out_first():
        o_ref[0] = x_ref[0] + _dot(ynat_buf[:, pool_cols], wout_ref[pool_cols, :])

    def out_second():
        o_ref[0] += _dot(ynat_buf[:, conv_cols], wout_ref[conv_cols, :])

    step(matmul_steps, "unperm_pool", 256, [("pool_gate", r) for r in range(len(ew_rows))], unpermute(pool_cols))
    step(matmul_steps, "out_pool", 1024, ["unperm_pool"], out_first)
    step(matmul_steps, "unperm_conv", 256, [("norm_gate", r) for r in range(len(ew_rows))], unpermute(conv_cols))
    step(matmul_steps, "out_conv", 1024, ["unperm_conv", "out_pool"], out_second)

    for c, cols in enumerate(col_chunks):
        def glu(cols=cols):
            last_strand_to_front(u_buf, cols)
            for rows in conv_rows:
                u_buf[ts + rows.start:ts + rows.stop, cols] = (
                    proj_buf[rows, cols] * _sigmoid(proj_buf[rows, shifted_cols(cols, B_W)]))
            finish_shift(u_buf, cols)
        step(vector_steps, ("glu", c), 200, [("bv", c), ("bg", c)], glu)
        for r, rows in enumerate(conv_rows):
            def conv(rows=rows, cols=cols):
                acc = jnp.broadcast_to(cb_ref[:, cols], (CONV_ROWS, CONV_COLS))
                for k in range(CONV_K):
                    acc = acc + cw_ref[k:k + 1, cols] * delayed(u_buf, CONV_K - 1 - k, rows, cols)
                conv_buf[rows, cols] = acc
            step(vector_steps, ("conv", c, r), 250, [("glu", c)], conv)

    for g, w in enumerate(POOL_WINDOWS):
        for r, rows in enumerate(conv_rows):
            def pool(g=g, w=w, rows=rows):
                cols = col_chunks[g]
                cur = delayed(a_buf, 0, rows, cols)
                win = cur
                for d in range(1, w):
                    win = win + delayed(a_buf, d, rows, cols)
                cnt = jnp.minimum(tpos[rows] + 1, w).astype(F32)
                pool_buf[rows, cols] = (win / cnt - cur).astype(BF16)
            step(vector_steps, ("pool", g, r), 30, [("a", g)], pool)
    for r, rows in enumerate(ew_rows):
        def pool_gate(rows=rows):
            y_buf[rows, pool_cols] = (mix_buf[rows, :] * ps_ref[...] * silu_z(rows, pool_cols)).astype(BF16)
        needs = [("pool_mix", g) for g in range(len(POOL_WINDOWS))] + [("z", j) for j in range(n_z // 2)]
        step(vector_steps, ("pool_gate", r), 80, needs, pool_gate)

    for r, rows in enumerate(ew_rows):
        def norm_gate(rows=rows):
            conv = conv_buf[rows, :]
            mu = jnp.mean(conv, axis=-1, keepdims=True)
            cen = conv - mu
            var = jnp.mean(cen * cen, axis=-1, keepdims=True)
            yn = cen * lax.rsqrt(var + EPS) * cg_ref[...] + cbeta_ref[...]
            y_buf[rows, conv_cols] = (yn * _sigmoid(yn) * silu_z(rows, shifted_cols(pool_cols, A_W))).astype(BF16)
        needs = ([("conv", c, q) for c in range(len(col_chunks)) for q in range(len(conv_rows))]
                 + [("z", j) for j in range(n_z // 2, n_z)])
        step(vector_steps, ("norm_gate", r), 190, needs, norm_gate)

    def emit(queues):
        done, spent, pos = set(), [0] * len(queues), [0] * len(queues)
        while any(pos[q] < len(queues[q]) for q in range(len(queues))):
            ready = [q for q in range(len(queues)) if pos[q] < len(queues[q])
                     and all(n in done for n in queues[q][pos[q]][2])]
            assert ready, "step ordering deadlock"
            q = min(ready, key=lambda q: spent[q])
            name, cost, _, fn = queues[q][pos[q]]
            fn()
            done.add(name)
            spent[q] += cost
            pos[q] += 1

    for rows in ew_rows:
        h_buf[rows, :] = _rms_norm_rows(x_ref[0, rows, :], ng_ref[...]).astype(BF16)
    hp_buf[...] = _dot(perm_ref[0], h_buf[...]).astype(BF16)
    emit([matmul_steps, vector_steps])


def _pool_conv_layer(x, norm_g, w_in, pool_w, pool_scale, conv_w, conv_b, cn_g, cn_b, w_out):
    B, S, D = x.shape
    ts = SEQ_TILE
    const2 = lambda b, i: (0, 0)
    const3 = lambda b, i: (0, 0, 0)
    single = pl.Buffered(1)
    return pl.pallas_call(
        _pool_conv_kernel,
        grid=(B, S // ts),
        in_specs=[
            pl.BlockSpec((1, ts, D), lambda b, i: (b, i, 0)),
            pl.BlockSpec((2, ts, ts), const3, pipeline_mode=single),
            pl.BlockSpec((1, D), const2),
            pl.BlockSpec((D, IN0_W), const2, pipeline_mode=single),
            pl.BlockSpec((len(POOL_WINDOWS), POOL_CH, POOL_CH), const3, pipeline_mode=single),
            pl.BlockSpec((1, A_W), const2),
            pl.BlockSpec((CONV_K, B_W), const2),
            pl.BlockSpec((1, B_W), const2),
            pl.BlockSpec((1, B_W), const2),
            pl.BlockSpec((1, B_W), const2),
            pl.BlockSpec((MIX_W, D), const2, pipeline_mode=single),
        ],
        out_specs=pl.BlockSpec((1, ts, D), lambda b, i: (b, i, 0)),
        out_shape=jax.ShapeDtypeStruct((B, S, D), F32),
        scratch_shapes=[
            pltpu.VMEM((2 * ts, A_W), F32),
            pltpu.VMEM((2 * ts, B_W), F32),
            pltpu.VMEM((ts, D), BF16),
            pltpu.VMEM((ts, D), BF16),
            pltpu.VMEM((ts, IN0_W - A_W), F32),
            pltpu.VMEM((ts, A_W), BF16),
            pltpu.VMEM((ts, A_W), F32),
            pltpu.VMEM((ts, B_W), F32),
            pltpu.VMEM((ts, MIX_W), BF16),
            pltpu.VMEM((ts, MIX_W), BF16),
        ],
        compiler_params=pltpu.CompilerParams(
            dimension_semantics=("arbitrary", "arbitrary"), vmem_limit_bytes=VMEM_LIMIT),
        name="pool_conv_layer",
    )(x, _strand_permutation(ts), norm_g.reshape(1, D), w_in.astype(BF16), pool_w.astype(BF16),
      pool_scale.reshape(1, A_W),
      conv_w, conv_b.reshape(1, B_W), cn_g.reshape(1, B_W), cn_b.reshape(1, B_W), w_out.astype(BF16))


def _head_rms_norm(t, g_row):
    low = lax.broadcasted_iota(jnp.int32, (1, LANES), 1) < HEAD_DIM
    cols = []
    for c in range(ATT_W // LANES):
        tc = t[:, c * LANES:(c + 1) * LANES]
        sq = tc * tc
        s_low = jnp.sum(jnp.where(low, sq, 0.0), axis=-1, keepdims=True)
        s_high = jnp.sum(jnp.where(low, 0.0, sq), axis=-1, keepdims=True)
        ms = jnp.where(low, s_low, s_high) * (1.0 / HEAD_DIM)
        cols.append(tc * lax.rsqrt(ms + EPS) * g_row[:, c * LANES:(c + 1) * LANES])
    return cols


def _moba_proj_kernel(x_ref, ng_ref, win_ref, qg_ref, kg_ref, q_ref, k_ref, vt_ref, gz_ref, km_ref):
    for n in range(PROJ_BLOCKS):
        rows = slice(n * MOBA_BLOCK, (n + 1) * MOBA_BLOCK)
        h = _rms_norm_rows(x_ref[0, rows, :], ng_ref[...]).astype(BF16)

        q = _dot(h, win_ref[:, 0:ATT_W])
        for c, qc in enumerate(_head_rms_norm(q, qg_ref[...])):
            q_ref[0, rows, c * LANES:(c + 1) * LANES] = (qc * (HEAD_DIM ** -0.5 * LOG2_E)).astype(BF16)

        k = _dot(h, win_ref[:, ATT_W:2 * ATT_W])
        for c, kc in enumerate(_head_rms_norm(k, kg_ref[...])):
            k_ref[0, rows, c * LANES:(c + 1) * LANES] = kc.astype(BF16)
            km_ref[0, n, :, c * LANES:(c + 1) * LANES] = jnp.mean(kc, axis=0, keepdims=True)

        v = _dot(h, win_ref[:, 2 * ATT_W:3 * ATT_W])
        vt_ref[0, n] = v.T.astype(BF16)

        z = _dot(h, win_ref[:, 3 * ATT_W:4 * ATT_W])
        gz_ref[0, rows, :] = (z * _sigmoid(z)).astype(BF16)


def _moba_proj(x, norm_g, w_in, q_norm_g, k_norm_g):
    B, S, D = x.shape
    blk = MOBA_BLOCK
    nb = S // blk
    ts = PROJ_BLOCKS * blk
    const2 = lambda b, i: (0, 0)
    row_blk = pl.BlockSpec((1, ts, ATT_W), lambda b, i: (b, i, 0))
    return pl.pallas_call(
        _moba_proj_kernel,
        grid=(B, S // ts),
        in_specs=[
            pl.BlockSpec((1, ts, D), lambda b, i: (b, i, 0)),
            pl.BlockSpec((1, D), const2),
            pl.BlockSpec((D, 4 * ATT_W), const2, pipeline_mode=pl.Buffered(1)),
            pl.BlockSpec((1, ATT_W), const2),
            pl.BlockSpec((1, ATT_W), const2),
        ],
        out_specs=[
            row_blk,
            row_blk,
            pl.BlockSpec((1, PROJ_BLOCKS, ATT_W, blk), lambda b, i: (b, i, 0, 0)),
            row_blk,
            pl.BlockSpec((1, PROJ_BLOCKS, 1, ATT_W), lambda b, i: (b, i, 0, 0)),
        ],
        out_shape=[
            jax.ShapeDtypeStruct((B, S, ATT_W), BF16),
            jax.ShapeDtypeStruct((B, S, ATT_W), BF16),
            jax.ShapeDtypeStruct((B, nb, ATT_W, blk), BF16),
            jax.ShapeDtypeStruct((B, S, ATT_W), BF16),
            jax.ShapeDtypeStruct((B, nb, 1, ATT_W), F32),
        ],
        compiler_params=pltpu.CompilerParams(
            dimension_semantics=("arbitrary", "arbitrary"), vmem_limit_bytes=VMEM_LIMIT),
        name="moba_proj",
    )(x, norm_g.reshape(1, D), w_in.astype(BF16),
      jnp.tile(q_norm_g, N_HEADS).reshape(1, ATT_W), jnp.tile(k_norm_g, N_HEADS).reshape(1, ATT_W))


def _moba_attn_kernel(slopes_ref, q_lo_ref, q_hi_ref, k_ref, vt_ref, km_ref, gz_lo_ref, gz_hi_ref,
                      o_lo_ref, o_hi_ref, qa_ref, bias_ref, *head_scratch):
    blk = MOBA_BLOCK
    n_h = ATTN_HEADS_PER_STEP
    n_sets = ATTN_SCRATCH_SETS
    s_refs = [head_scratch[h % n_sets] for h in range(n_h)]
    p_refs = [head_scratch[n_sets + h % n_sets] for h in range(n_h)]
    heads = range(n_h)

    def lanes_of(h):
        g0 = (h // HEADS_PER_GROUP) * LANES
        return slice(g0, g0 + LANES)
    nb = km_ref.shape[1]
    half = nb // 2
    p = pl.program_id(1)
    step = pl.program_id(2)
    neg_inf = jnp.float32(-jnp.inf)
    LO, HI = 0, 1
    q_blk = (step, nb - 1 - step)
    q_refs = (q_lo_ref, q_hi_ref)

    km = km_ref[0]
    row = lax.broadcasted_iota(jnp.int32, (LANES, 1), 0)
    lane = lax.broadcasted_iota(jnp.int32, (1, LANES), 1)
    blk_id = lax.broadcasted_iota(jnp.int32, (nb, blk), 0)
    blk_f = blk_id.astype(F32)
    key_off = lax.broadcasted_iota(jnp.int32, (blk, LANES), 0).astype(F32)
    k_extra = jnp.where(lane < SLOPE_TERMS, key_off, 0.0).astype(BF16)
    causal = (lax.broadcasted_iota(jnp.int32, (blk, blk), 1)
              >= lax.broadcasted_iota(jnp.int32, (blk, blk), 0))

    def fold8(t, op):
        parts = [t[r:r + SUBLANES] for r in range(0, t.shape[0], SUBLANES)]
        while len(parts) > 1:
            parts = [op(parts[a], parts[a + 1]) for a in range(0, len(parts), 2)]
        return parts[0]

    cslope = []
    for h in heads:
        hh = h % HEADS_PER_GROUP
        cs = slopes_ref[p * n_h + h] * LOG2_E
        cslope.append(cs)
        cs_v = jnp.full((LANES, blk), cs, F32)
        cs_hi = cs_v.astype(BF16).astype(F32)
        cs_mid = (cs_v - cs_hi).astype(BF16).astype(F32)
        cs_lo = cs_v - cs_hi - cs_mid
        q_extra = jnp.where(row == 0, cs_hi, jnp.where(row == 1, cs_mid, jnp.where(row == 2, cs_lo, 0.0)))
        q_extra = q_extra.astype(BF16)
        for qb in (LO, HI):
            q_t = q_refs[qb][0, :, lanes_of(h)].astype(F32).T
            q_head = jnp.where(row // HEAD_DIM == hh, q_t, 0.0).astype(BF16)
            qa_ref[qb, h] = jnp.concatenate([q_head, q_extra], axis=0)

    def choose_blocks(h, qb):
        i = q_blk[qb]
        kmh = jnp.where(lane // HEAD_DIM == h % HEADS_PER_GROUP, km[:, lanes_of(h)], 0.0)
        km_hi = kmh.astype(BF16)
        km_lo = (kmh - km_hi.astype(F32)).astype(BF16)
        q_head = qa_ref[qb, h, 0:LANES, :]
        gate = _dot(km_hi, q_head) + _dot(km_lo, q_head)
        avail = jnp.where(blk_id < i, 1.0, 0.0)
        keep = jnp.zeros((nb, blk), F32)
        for _ in range(MOBA_TOPK):
            g = jnp.where(avail > 0.0, gate, neg_inf)
            top = jnp.max(g, axis=0, keepdims=True)
            first = jnp.min(jnp.where((g == top) & (avail > 0.0), blk_f, float(nb)), axis=0, keepdims=True)
            pick = jnp.where(blk_f == first, 1.0, 0.0)
            keep = keep + pick
            avail = avail - pick
        dist = ((blk_id - i) * blk).astype(F32)
        bias_ref[qb, h] = jnp.where(keep > 0.0, cslope[h] * dist, jnp.where(blk_id == i, 0.0, neg_inf))

    def key_rows(j):
        return pl.ds(pl.multiple_of(j * blk, blk), blk)

    n_slots = nb + 1
    slot_is_lo, slot_key, slot_causal = [], [], []
    for t in range(n_slots):
        if t <= half:
            slot_is_lo.append(None)
            slot_key.append(q_blk[HI] - half + t)
            slot_causal.append(t == half)
        else:
            u = t - (half + 1)
            is_lo = u >= q_blk[HI] - half
            slot_is_lo.append(is_lo)
            slot_key.append(jnp.where(is_lo, u - (q_blk[HI] - half), u))
            slot_causal.append(t == n_slots - 1)

    def pick(is_lo, lo, hi):
        return hi if is_lo is None else jnp.where(is_lo, lo, hi)

    def bias_row(t, hh):
        qb = HI if slot_is_lo[t] is None else jnp.where(slot_is_lo[t], LO, HI)
        return bias_ref[qb, hh, pl.ds(slot_key[t], 1), :]


    m8 = [[jnp.full((SUBLANES, blk), neg_inf, F32) for _ in heads] for _ in (LO, HI)]
    m_row = [[None] * n_h for _ in (LO, HI)]
    acc = [[None] * n_h for _ in (LO, HI)]
    norm = [[None] * n_h for _ in (LO, HI)]
    n_fixed = half + 1

    def pass1(t, hh):
        is_lo = slot_is_lo[t]
        k_aug = jnp.concatenate([k_ref[0, key_rows(slot_key[t]), lanes_of(hh)], k_extra], axis=1)
        qa = qa_ref[HI, hh] if is_lo is None else qa_ref[jnp.where(is_lo, LO, HI), hh]
        s = _dot(k_aug, qa)
        if slot_causal[t]:
            s = jnp.where(causal, s, neg_inf)
        s_refs[hh][t] = s
        top = fold8(s, jnp.maximum) + bias_row(t, hh)
        if is_lo is None:
            m8[HI][hh] = jnp.maximum(m8[HI][hh], top)
        else:
            m8[LO][hh] = jnp.maximum(m8[LO][hh], jnp.where(is_lo, top, neg_inf))
            m8[HI][hh] = jnp.maximum(m8[HI][hh], jnp.where(is_lo, neg_inf, top))

    def row_max(hh):
        for qb in (LO, HI):
            m_row[qb][hh] = jnp.max(m8[qb][hh], axis=0, keepdims=True)

    def pass2(t, hh):
        shift = pick(slot_is_lo[t], m_row[LO][hh], m_row[HI][hh]) - bias_row(t, hh)
        p_refs[hh][t] = jnp.exp2(s_refs[hh][t] - shift).astype(BF16)

    ones_rows = jnp.ones((BF16_ROWS, blk), F32)
    v_rows = HEAD_DIM + BF16_ROWS

    def values_and_ones(t, hh):
        return jnp.concatenate(
            [vt_ref[0, slot_key[t], hh * HEAD_DIM:(hh + 1) * HEAD_DIM, :].astype(F32), ones_rows], axis=0)

    pv_fixed = [None] * n_h

    def value_product_fixed(hh):
        v_fixed = jnp.concatenate([values_and_ones(t, hh).astype(BF16) for t in range(n_fixed)], axis=1)
        pv_fixed[hh] = _dot(v_fixed, p_refs[hh][0:n_fixed].reshape(n_fixed * blk, blk))

    def value_product_split(hh):
        pieces = []
        for t in range(n_fixed, n_slots):
            v32 = values_and_ones(t, hh)
            pieces.append(jnp.concatenate([jnp.where(slot_is_lo[t], v32, 0.0),
                                           jnp.where(slot_is_lo[t], 0.0, v32)], axis=0).astype(BF16))
        v_split = jnp.concatenate(pieces, axis=1)
        pv_split = _dot(v_split, p_refs[hh][n_fixed:n_slots].reshape((n_slots - n_fixed) * blk, blk))
        pv_hi = pv_fixed[hh] + pv_split[v_rows:2 * v_rows]
        acc[LO][hh] = pv_split[0:HEAD_DIM]
        norm[LO][hh] = pv_split[HEAD_DIM:HEAD_DIM + 1]
        acc[HI][hh] = pv_hi[0:HEAD_DIM]
        norm[HI][hh] = pv_hi[HEAD_DIM:HEAD_DIM + 1]

    for h in heads:
        for qb in (LO, HI):
            choose_blocks(h, qb)
    for t in range(n_slots):
        pass1(t, 0)
    row_max(0)
    for h in heads:
        for t in range(n_slots):
            pass2(t, h)
            if h + 1 < n_h:
                pass1(t, h + 1)
            if t == 2 and h > 0:
                value_product_split(h - 1)
            if t == n_fixed:
                value_product_fixed(h)
        if h + 1 < n_h:
            row_max(h + 1)
    value_product_split(n_h - 1)

    for qb, o_ref, gz_ref in ((LO, o_lo_ref, gz_lo_ref), (HI, o_hi_ref, gz_hi_ref)):
        o_t = jnp.concatenate([acc[qb][hh] / norm[qb][hh] for hh in heads], axis=0)
        o_ref[0] = (o_t.T * gz_ref[0].astype(F32)).astype(BF16)


def _moba_attn(q, k, vt, kmean, gz, slopes):
    B, S, _ = q.shape
    blk = MOBA_BLOCK
    nb = S // blk
    n_h = ATTN_HEADS_PER_STEP
    width = n_h * HEAD_DIM
    half = nb // 2
    lo_blk = pl.BlockSpec((1, blk, width), lambda b, p, s: (b, s, p))
    hi_blk = pl.BlockSpec((1, blk, width), lambda b, p, s: (b, nb - 1 - s, p))
    return pl.pallas_call(
        _moba_attn_kernel,
        grid=(B, ATT_W // width, half),
        in_specs=[
            pl.BlockSpec(memory_space=pltpu.SMEM),
            lo_blk,
            hi_blk,
            pl.BlockSpec((1, S, width), lambda b, p, s: (b, 0, p)),
            pl.BlockSpec((1, nb, width, blk), lambda b, p, s: (b, 0, p, 0)),
            pl.BlockSpec((1, nb, width), lambda b, p, s: (b, 0, p)),
            lo_blk,
            hi_blk,
        ],
        out_specs=[
            lo_blk,
            pl.BlockSpec((1, blk, width), lambda b, p, s: (b, half - 1 - s, p)),
        ],
        out_shape=[jax.ShapeDtypeStruct((B, S // 2, ATT_W), BF16)] * 2,
        scratch_shapes=[
            pltpu.VMEM((2, n_h, 2 * LANES, blk), BF16),
            pltpu.VMEM((2, n_h, nb, blk), F32),
        ] + [pltpu.VMEM((nb + 1, blk, blk), F32)] * ATTN_SCRATCH_SETS
          + [pltpu.VMEM((nb + 1, blk, blk), BF16)] * ATTN_SCRATCH_SETS,
        compiler_params=pltpu.CompilerParams(
            dimension_semantics=("arbitrary", "arbitrary", "arbitrary"), vmem_limit_bytes=VMEM_LIMIT),
        name="moba_attn",
    )(slopes, q, q, k, vt, kmean, gz, gz)


def _moba_out_kernel(x_hbm, og_lo_hbm, og_hi_hbm, w_ref, o_ref, x_buf, og_buf, x_sem, og_sem,
                     *, tiles_per_seq, n_tiles):
    ts = x_buf.shape[1]
    per_half = tiles_per_seq // 2
    n = pl.program_id(0)

    def x_copy(t, slot):
        b, i = t // tiles_per_seq, t % tiles_per_seq
        return pltpu.make_async_copy(x_hbm.at[b, pl.ds(i * ts, ts), :], x_buf.at[slot], x_sem.at[slot])

    def og_copy(t, slot, second_half):
        b, i = t // tiles_per_seq, t % tiles_per_seq
        src = og_hi_hbm if second_half else og_lo_hbm
        j = jnp.maximum(i - per_half, 0) if second_half else jnp.minimum(i, per_half - 1)
        return pltpu.make_async_copy(src.at[b, pl.ds(j * ts, ts), :], og_buf.at[slot], og_sem.at[slot])

    def for_tile(t, slot, act):
        act(x_copy(t, slot))
        second_half = t % tiles_per_seq >= per_half
        if isinstance(t, int):
            act(og_copy(t, slot, second_half))
        else:
            pl.when(jnp.logical_not(second_half))(lambda: act(og_copy(t, slot, False)))
            pl.when(second_half)(lambda: act(og_copy(t, slot, True)))

    @pl.when(n == 0)
    def _():
        for t in range(OUT_RING):
            for_tile(t, t, lambda c: c.start())

    slot = n % OUT_RING
    for_tile(n, slot, lambda c: c.wait())
    o_ref[0] = x_buf[slot] + _dot(og_buf[slot], w_ref[...])

    @pl.when(n + OUT_RING < n_tiles)
    def _():
        for_tile(n + OUT_RING, slot, lambda c: c.start())


def _moba_out(x, og_lo, og_hi, w_out):
    B, S, D = x.shape
    ts = 4 * SEQ_TILE
    tiles_per_seq = S // ts
    n_tiles = B * tiles_per_seq
    assert tiles_per_seq % 2 == 0 and n_tiles >= OUT_RING
    hbm = pl.BlockSpec(memory_space=pl.ANY)
    return pl.pallas_call(
        functools.partial(_moba_out_kernel, tiles_per_seq=tiles_per_seq, n_tiles=n_tiles),
        grid=(n_tiles,),
        in_specs=[hbm, hbm, hbm, pl.BlockSpec((ATT_W, D), lambda n: (0, 0), pipeline_mode=pl.Buffered(1))],
        out_specs=pl.BlockSpec((1, ts, D), lambda n: (n // tiles_per_seq, n % tiles_per_seq, 0)),
        out_shape=jax.ShapeDtypeStruct((B, S, D), F32),
        scratch_shapes=[
            pltpu.VMEM((OUT_RING, ts, D), F32),
            pltpu.VMEM((OUT_RING, ts, ATT_W), BF16),
            pltpu.SemaphoreType.DMA((OUT_RING,)),
            pltpu.SemaphoreType.DMA((OUT_RING,)),
        ],
        compiler_params=pltpu.CompilerParams(
            dimension_semantics=("arbitrary",), vmem_limit_bytes=VMEM_LIMIT),
        name="moba_out",
    )(x, og_lo, og_hi, w_out.astype(BF16))


def kernel(x, norm0_g, w_in0, pool_w, pool_scale, conv_w, conv_b, cnorm_g, cnorm_b, w_out0,
           norm1_g, w_in1, q_norm_g, k_norm_g, w_out1):
    assert x.shape[1] % MOBA_BLOCK == 0 and x.shape[2] == D_MODEL
    x = _pool_conv_layer(x, norm0_g[0], w_in0[0], pool_w[0], pool_scale[0], conv_w[0], conv_b[0],
                         cnorm_g[0], cnorm_b[0], w_out0[0])
    q, k, vt, gz, kmean = _moba_proj(x, norm1_g[0], w_in1[0], q_norm_g[0], k_norm_g[0])
    slopes = jnp.exp2(-8.0 * jnp.arange(1, N_HEADS + 1, dtype=F32) / N_HEADS)
    og_lo, og_hi = _moba_attn(q, k, vt, kmean.reshape(kmean.shape[0], -1, ATT_W), gz, slopes)
    return _moba_out(x, og_lo, og_hi, w_out1[0])
```

```python
import functools
import math

import jax
import jax.numpy as jnp
import numpy as np
from jax import lax
from jax.experimental import pallas as pl
from jax.experimental.pallas import tpu as pltpu

D_MODEL = 1024
A_W = 1024
B_W = 1024
MIX_W = 2048
POOL_WINDOWS = (2, 4, 8, 16)
POOL_CH = 256
CONV_K = 31
IN0_W = A_W + 2 * B_W + MIX_W
HEAD_DIM = 64
N_HEADS = 16
ATT_W = 1024
MOBA_BLOCK = 256
MOBA_TOPK = 3
EPS = 1e-6
LOG2_E = math.log2(math.e)

SEQ_TILE = 256
OUT_RING = 4
PROJ_BLOCKS = 4
EW_ROWS = 32
CONV_ROWS = 64
CONV_COLS = 256
LANES = 128
SUBLANES = 8
BF16_ROWS = 16
HEADS_PER_GROUP = LANES // HEAD_DIM
ATTN_HEADS_PER_STEP = 8
ATTN_SCRATCH_SETS = 3
SLOPE_TERMS = 3
VMEM_LIMIT = 56 * 1024 * 1024

F32 = jnp.float32
BF16 = jnp.bfloat16


def _dot(a, b):
    return jnp.dot(a, b, preferred_element_type=F32)


def _rms_norm_rows(x, g):
    ms = jnp.mean(x * x, axis=-1, keepdims=True)
    return x * lax.rsqrt(ms + EPS) * g


def _sigmoid(x):
    return 1.0 / (1.0 + jnp.exp(-x))


def _strand_permutation(ts):
    r = np.arange(ts)
    time_of_row = (r % SUBLANES) * (ts // SUBLANES) + r // SUBLANES
    to_perm = (r[None, :] == time_of_row[:, None]).astype(np.float32)
    return jnp.asarray(np.stack([to_perm, to_perm.T]), BF16)


def _pool_conv_kernel(x_ref, perm_ref, ng_ref, win_ref, pw_ref, ps_ref, cw_ref, cb_ref, cg_ref, cbeta_ref,
                      wout_ref, o_ref, a_buf, u_buf, h_buf, hp_buf, proj_buf, pool_buf, mix_buf, conv_buf,
                      y_buf, ynat_buf):
    ts = SEQ_TILE
    strand = ts // SUBLANES
    i = pl.program_id(1)

    row = lax.broadcasted_iota(jnp.int32, (ts, 1), 0)
    first_strand = row % SUBLANES == 0
    tpos = i * ts + (row % SUBLANES) * strand + row // SUBLANES

    @pl.when(i == 0)
    def _():
        a_buf[ts:2 * ts, :] = jnp.zeros((ts, A_W), F32)
        u_buf[ts:2 * ts, :] = jnp.zeros((ts, B_W), F32)

    def last_strand_to_front(buf, cols):
        buf[0:ts, cols] = pltpu.roll(buf[ts:2 * ts, cols], ts - (SUBLANES - 1), 0)

    def finish_shift(buf, cols):
        buf[0:ts, cols] = jnp.where(first_strand, buf[0:ts, cols], pltpu.roll(buf[ts:2 * ts, cols], 1, 0))

    def delayed(buf, d, rows, cols):
        start = (strand - d) * SUBLANES + rows.start
        return buf[start:start + (rows.stop - rows.start), cols]

    def row_chunks(n):
        return [slice(r, r + n) for r in range(0, ts, n)]

    def shifted_cols(cols, by):
        return slice(cols.start + by, cols.stop + by)

    def silu_z(rows, cols):
        z = proj_buf[rows, shifted_cols(cols, 2 * B_W)]
        return z * _sigmoid(z)

    col_chunks = [slice(c, c + CONV_COLS) for c in range(0, B_W, CONV_COLS)]
    conv_rows = row_chunks(CONV_ROWS)
    ew_rows = row_chunks(EW_ROWS)
    matmul_steps, vector_steps = [], []

    def step(queue, name, cost, needs, fn):
        queue.append((name, cost, tuple(needs), fn))

    def in_proj(c0):
        def run():
            proj_buf[:, c0:c0 + CONV_COLS] = _dot(hp_buf[...], win_ref[:, A_W + c0:A_W + c0 + CONV_COLS])
        return run

    n_z = MIX_W // CONV_COLS
    for c, cols in enumerate(col_chunks):
        step(matmul_steps, ("bv", c), 256, [], in_proj(cols.start))
        step(matmul_steps, ("bg", c), 256, [], in_proj(B_W + cols.start))
    for c, cols in enumerate(col_chunks):
        def pool_in(cols=cols):
            last_strand_to_front(a_buf, cols)
            a_buf[ts:2 * ts, cols] = _dot(hp_buf[...], win_ref[:, cols])
            finish_shift(a_buf, cols)
        step(matmul_steps, ("a", c), 256, [], pool_in)
    for j in range(n_z):
        step(matmul_steps, ("z", j), 256, [], in_proj(2 * B_W + j * CONV_COLS))
    for g in range(len(POOL_WINDOWS)):
        def pool_mix(g=g):
            gc = col_chunks[g]
            mix_buf[:, gc] = _dot(pool_buf[:, gc], pw_ref[g])
        step(matmul_steps, ("pool_mix", g), 64, [("pool", g, r) for r in range(len(conv_rows))], pool_mix)

    def unpermute(cols):
        def run():
            ynat_buf[:, cols] = _dot(perm_ref[1], y_buf[:, cols]).astype(BF16)
        return run

    pool_cols, conv_cols = slice(0, A_W), slice(A_W, MIX_W)

    def out_first():
        o_ref[0] = x_ref[0] + _dot(ynat_buf[:, pool_cols], wout_ref[pool_cols, :])

    def out_second():
        o_ref[0] += _dot(ynat_buf[:, conv_cols], wout_ref[conv_cols, :])

    step(matmul_steps, "unperm_pool", 256, [("pool_gate", r) for r in range(len(ew_rows))], unpermute(pool_cols))
    step(matmul_steps, "out_pool", 1024, ["unperm_pool"], out_first)
    step(matmul_steps, "unperm_conv", 256, [("norm_gate", r) for r in range(len(ew_rows))], unpermute(conv_cols))
    step(matmul_steps, "out_conv", 1024, ["unperm_conv", "out_pool"], out_second)

    for c, cols in enumerate(col_chunks):
        def glu(cols=cols):
            last_strand_to_front(u_buf, cols)
            for rows in conv_rows:
                u_buf[ts + rows.start:ts + rows.stop, cols] = (
                    proj_buf[rows, cols] * _sigmoid(proj_buf[rows, shifted_cols(cols, B_W)]))
            finish_shift(u_buf, cols)
        step(vector_steps, ("glu", c), 200, [("bv", c), ("bg", c)], glu)
        for r, rows in enumerate(conv_rows):
            def conv(rows=rows, cols=cols):
                acc = jnp.broadcast_to(cb_ref[:, cols], (CONV_ROWS, CONV_COLS))
                for k in range(CONV_K):
                    acc = acc + cw_ref[k:k + 1, cols] * delayed(u_buf, CONV_K - 1 - k, rows, cols)
                conv_buf[rows, cols] = acc
            step(vector_steps, ("conv", c, r), 250, [("glu", c)], conv)

    for g, w in enumerate(POOL_WINDOWS):
        for r, rows in enumerate(conv_rows):
            def pool(g=g, w=w, rows=rows):
                cols = col_chunks[g]
                cur = delayed(a_buf, 0, rows, cols)
                win = cur
                for d in range(1, w):
                    win = win + delayed(a_buf, d, rows, cols)
                cnt = jnp.minimum(tpos[rows] + 1, w).astype(F32)
                pool_buf[rows, cols] = (win / cnt - cur).astype(BF16)
            step(vector_steps, ("pool", g, r), 30, [("a", g)], pool)
    for r, rows in enumerate(ew_rows):
        def pool_gate(rows=rows):
            y_buf[rows, pool_cols] = (mix_buf[rows, :] * ps_ref[...] * silu_z(rows, pool_cols)).astype(BF16)
        needs = [("pool_mix", g) for g in range(len(POOL_WINDOWS))] + [("z", j) for j in range(n_z // 2)]
        step(vector_steps, ("pool_gate", r), 80, needs, pool_gate)

    for r, rows in enumerate(ew_rows):
        def norm_gate(rows=rows):
            conv = conv_buf[rows, :]
            mu = jnp.mean(conv, axis=-1, keepdims=True)
            cen = conv - mu
            var = jnp.mean(cen * cen, axis=-1, keepdims=True)
            yn = cen * lax.rsqrt(var + EPS) * cg_ref[...] + cbeta_ref[...]
            y_buf[rows, conv_cols] = (yn * _sigmoid(yn) * silu_z(rows, shifted_cols(pool_cols, A_W))).astype(BF16)
        needs = ([("conv", c, q) for c in range(len(col_chunks)) for q in range(len(conv_rows))]
                 + [("z", j) for j in range(n_z // 2, n_z)])
        step(vector_steps, ("norm_gate", r), 190, needs, norm_gate)

    def emit(queues):
        done, spent, pos = set(), [0] * len(queues), [0] * len(queues)
        while any(pos[q] < len(queues[q]) for q in range(len(queues))):
            ready = [q for q in range(len(queues)) if pos[q] < len(queues[q])
                     and all(n in done for n in queues[q][pos[q]][2])]
            assert ready, "step ordering deadlock"
            q = min(ready, key=lambda q: spent[q])
            name, cost, _, fn = queues[q][pos[q]]
            fn()
            done.add(name)
            spent[q] += cost
            pos[q] += 1

    for rows in ew_rows:
        h_buf[rows, :] = _rms_norm_rows(x_ref[0, rows, :], ng_ref[...]).astype(BF16)
    hp_buf[...] = _dot(perm_ref[0], h_buf[...]).astype(BF16)
    emit([matmul_steps, vector_steps])


def _pool_conv_layer(x, norm_g, w_in, pool_w, pool_scale, conv_w, conv_b, cn_g, cn_b, w_out):
    B, S, D = x.shape
    ts = SEQ_TILE
    const2 = lambda b, i: (0, 0)
    const3 = lambda b, i: (0, 0, 0)
    single = pl.Buffered(1)
    return pl.pallas_call(
        _pool_conv_kernel,
        grid=(B, S // ts),
        in_specs=[
            pl.BlockSpec((1, ts, D), lambda b, i: (b, i, 0)),
            pl.BlockSpec((2, ts, ts), const3, pipeline_mode=single),
            pl.BlockSpec((1, D), const2),
            pl.BlockSpec((D, IN0_W), const2, pipeline_mode=single),
            pl.BlockSpec((len(POOL_WINDOWS), POOL_CH, POOL_CH), const3, pipeline_mode=single),
            pl.BlockSpec((1, A_W), const2),
            pl.BlockSpec((CONV_K, B_W), const2),
            pl.BlockSpec((1, B_W), const2),
            pl.BlockSpec((1, B_W), const2),
            pl.BlockSpec((1, B_W), const2),
            pl.BlockSpec((MIX_W, D), const2, pipeline_mode=single),
        ],
        out_specs=pl.BlockSpec((1, ts, D), lambda b, i: (b, i, 0)),
        out_shape=jax.ShapeDtypeStruct((B, S, D), F32),
        scratch_shapes=[
            pltpu.VMEM((2 * ts, A_W), F32),
            pltpu.VMEM((2 * ts, B_W), F32),
            pltpu.VMEM((ts, D), BF16),
            pltpu.VMEM((ts, D), BF16),
            pltpu.VMEM((ts, IN0_W - A_W), F32),
            pltpu.VMEM((ts, A_W), BF16),
            pltpu.VMEM((ts, A_W), F32),
            pltpu.VMEM((ts, B_W), F32),
            pltpu.VMEM((ts, MIX_W), BF16),
            pltpu.VMEM((ts, MIX_W), BF16),
        ],
        compiler_params=pltpu.CompilerParams(
            dimension_semantics=("arbitrary", "arbitrary"), vmem_limit_bytes=VMEM_LIMIT),
        name="pool_conv_layer",
    )(x, _strand_permutation(ts), norm_g.reshape(1, D), w_in.astype(BF16), pool_w.astype(BF16),
      pool_scale.reshape(1, A_W),
      conv_w, conv_b.reshape(1, B_W), cn_g.reshape(1, B_W), cn_b.reshape(1, B_W), w_out.astype(BF16))


def _head_rms_norm(t, g_row):
    low = lax.broadcasted_iota(jnp.int32, (1, LANES), 1) < HEAD_DIM
    cols = []
    for c in range(ATT_W // LANES):
        tc = t[:, c * LANES:(c + 1) * LANES]
        sq = tc * tc
        s_low = jnp.sum(jnp.where(low, sq, 0.0), axis=-1, keepdims=True)
        s_high = jnp.sum(jnp.where(low, 0.0, sq), axis=-1, keepdims=True)
        ms = jnp.where(low, s_low, s_high) * (1.0 / HEAD_DIM)
        cols.append(tc * lax.rsqrt(ms + EPS) * g_row[:, c * LANES:(c + 1) * LANES])
    return cols


def _moba_proj_kernel(x_ref, ng_ref, win_ref, qg_ref, kg_ref, q_ref, k_ref, vt_ref, gz_ref, km_ref):
    for n in range(PROJ_BLOCKS):
        rows = slice(n * MOBA_BLOCK, (n + 1) * MOBA_BLOCK)
        h = _rms_norm_rows(x_ref[0, rows, :], ng_ref[...]).astype(BF16)

        q = _dot(h, win_ref[:, 0:ATT_W])
        for c, qc in enumerate(_head_rms_norm(q, qg_ref[...])):
            q_ref[0, rows, c * LANES:(c + 1) * LANES] = (qc * (HEAD_DIM ** -0.5 * LOG2_E)).astype(BF16)

        k = _dot(h, win_ref[:, ATT_W:2 * ATT_W])
        for c, kc in enumerate(_head_rms_norm(k, kg_ref[...])):
            k_ref[0, rows, c * LANES:(c + 1) * LANES] = kc.astype(BF16)
            km_ref[0, n, :, c * LANES:(c + 1) * LANES] = jnp.mean(kc, axis=0, keepdims=True)

        v = _dot(h, win_ref[:, 2 * ATT_W:3 * ATT_W])
        vt_ref[0, n] = v.T.astype(BF16)

        z = _dot(h, win_ref[:, 3 * ATT_W:4 * ATT_W])
        gz_ref[0, rows, :] = (z * _sigmoid(z)).astype(BF16)


def _moba_proj(x, norm_g, w_in, q_norm_g, k_norm_g):
    B, S, D = x.shape
    blk = MOBA_BLOCK
    nb = S // blk
    ts = PROJ_BLOCKS * blk
    const2 = lambda b, i: (0, 0)
    row_blk = pl.BlockSpec((1, ts, ATT_W), lambda b, i: (b, i, 0))
    return pl.pallas_call(
        _moba_proj_kernel,
        grid=(B, S // ts),
        in_specs=[
            pl.BlockSpec((1, ts, D), lambda b, i: (b, i, 0)),
            pl.BlockSpec((1, D), const2),
            pl.BlockSpec((D, 4 * ATT_W), const2, pipeline_mode=pl.Buffered(1)),
            pl.BlockSpec((1, ATT_W), const2),
            pl.BlockSpec((1, ATT_W), const2),
        ],
        out_specs=[
            row_blk,
            row_blk,
            pl.BlockSpec((1, PROJ_BLOCKS, ATT_W, blk), lambda b, i: (b, i, 0, 0)),
            row_blk,
            pl.BlockSpec((1, PROJ_BLOCKS, 1, ATT_W), lambda b, i: (b, i, 0, 0)),
        ],
        out_shape=[
            jax.ShapeDtypeStruct((B, S, ATT_W), BF16),
            jax.ShapeDtypeStruct((B, S, ATT_W), BF16),
            jax.ShapeDtypeStruct((B, nb, ATT_W, blk), BF16),
            jax.ShapeDtypeStruct((B, S, ATT_W), BF16),
            jax.ShapeDtypeStruct((B, nb, 1, ATT_W), F32),
        ],
        compiler_params=pltpu.CompilerParams(
            dimension_semantics=("arbitrary", "arbitrary"), vmem_limit_bytes=VMEM_LIMIT),
        name="moba_proj",
    )(x, norm_g.reshape(1, D), w_in.astype(BF16),
      jnp.tile(q_norm_g, N_HEADS).reshape(1, ATT_W), jnp.tile(k_norm_g, N_HEADS).reshape(1, ATT_W))


def _moba_attn_kernel(slopes_ref, q_lo_ref, q_hi_ref, k_ref, vt_ref, km_ref, gz_lo_ref, gz_hi_ref,
                      o_lo_ref, o_hi_ref, qa_ref, bias_ref, *head_scratch):
    blk = MOBA_BLOCK
    n_h = ATTN_HEADS_PER_STEP
    n_sets = ATTN_SCRATCH_SETS
    s_refs = [head_scratch[h % n_sets] for h in range(n_h)]
    p_refs = [head_scratch[n_sets + h % n_sets] for h in range(n_h)]
    heads = range(n_h)

    def lanes_of(h):
        g0 = (h // HEADS_PER_GROUP) * LANES
        return slice(g0, g0 + LANES)
    nb = km_ref.shape[1]
    half = nb // 2
    p = pl.program_id(1)
    step = pl.program_id(2)
    neg_inf = jnp.float32(-jnp.inf)
    LO, HI = 0, 1
    q_blk = (step, nb - 1 - step)
    q_refs = (q_lo_ref, q_hi_ref)

    km = km_ref[0]
    row = lax.broadcasted_iota(jnp.int32, (LANES, 1), 0)
    lane = lax.broadcasted_iota(jnp.int32, (1, LANES), 1)
    blk_id = lax.broadcasted_iota(jnp.int32, (nb, blk), 0)
    blk_f = blk_id.astype(F32)
    key_off = lax.broadcasted_iota(jnp.int32, (blk, LANES), 0).astype(F32)
    k_extra = jnp.where(lane < SLOPE_TERMS, key_off, 0.0).astype(BF16)
    causal = (lax.broadcasted_iota(jnp.int32, (blk, blk), 1)
              >= lax.broadcasted_iota(jnp.int32, (blk, blk), 0))

    def fold8(t, op):
        parts = [t[r:r + SUBLANES] for r in range(0, t.shape[0], SUBLANES)]
        while len(parts) > 1:
            parts = [op(parts[a], parts[a + 1]) for a in range(0, len(parts), 2)]
        return parts[0]

    cslope = []
    for h in heads:
        hh = h % HEADS_PER_GROUP
        cs = slopes_ref[p * n_h + h] * LOG2_E
        cslope.append(cs)
        cs_v = jnp.full((LANES, blk), cs, F32)
        cs_hi = cs_v.astype(BF16).astype(F32)
        cs_mid = (cs_v - cs_hi).astype(BF16).astype(F32)
        cs_lo = cs_v - cs_hi - cs_mid
        q_extra = jnp.where(row == 0, cs_hi, jnp.where(row == 1, cs_mid, jnp.where(row == 2, cs_lo, 0.0)))
        q_extra = q_extra.astype(BF16)
        for qb in (LO, HI):
            q_t = q_refs[qb][0, :, lanes_of(h)].astype(F32).T
            q_head = jnp.where(row // HEAD_DIM == hh, q_t, 0.0).astype(BF16)
            qa_ref[qb, h] = jnp.concatenate([q_head, q_extra], axis=0)

    def choose_blocks(h, qb):
        i = q_blk[qb]
        kmh = jnp.where(lane // HEAD_DIM == h % HEADS_PER_GROUP, km[:, lanes_of(h)], 0.0)
        km_hi = kmh.astype(BF16)
        km_lo = (kmh - km_hi.astype(F32)).astype(BF16)
        q_head = qa_ref[qb, h, 0:LANES, :]
        gate = _dot(km_hi, q_head) + _dot(km_lo, q_head)
        avail = jnp.where(blk_id < i, 1.0, 0.0)
        keep = jnp.zeros((nb, blk), F32)
        for _ in range(MOBA_TOPK):
            g = jnp.where(avail > 0.0, gate, neg_inf)
            top = jnp.max(g, axis=0, keepdims=True)
            first = jnp.min(jnp.where((g == top) & (avail > 0.0), blk_f, float(nb)), axis=0, keepdims=True)
            pick = jnp.where(blk_f == first, 1.0, 0.0)
            keep = keep + pick
            avail = avail - pick
        dist = ((blk_id - i) * blk).astype(F32)
        bias_ref[qb, h] = jnp.where(keep > 0.0, cslope[h] * dist, jnp.where(blk_id == i, 0.0, neg_inf))

    def key_rows(j):
        return pl.ds(pl.multiple_of(j * blk, blk), blk)

    n_slots = nb + 1
    slot_is_lo, slot_key, slot_causal = [], [], []
    for t in range(n_slots):
        if t <= half:
            slot_is_lo.append(None)
            slot_key.append(q_blk[HI] - half + t)
            slot_causal.append(t == half)
        else:
            u = t - (half + 1)
            is_lo = u >= q_blk[HI] - half
            slot_is_lo.append(is_lo)
            slot_key.append(jnp.where(is_lo, u - (q_blk[HI] - half), u))
            slot_causal.append(t == n_slots - 1)

    def pick(is_lo, lo, hi):
        return hi if is_lo is None else jnp.where(is_lo, lo, hi)

    def bias_row(t, hh):
        qb = HI if slot_is_lo[t] is None else jnp.where(slot_is_lo[t], LO, HI)
        return bias_ref[qb, hh, pl.ds(slot_key[t], 1), :]


    m8 = [[jnp.full((SUBLANES, blk), neg_inf, F32) for _ in heads] for _ in (LO, HI)]
    m_row = [[None] * n_h for _ in (LO, HI)]
    acc = [[None] * n_h for _ in (LO, HI)]
    norm = [[None] * n_h for _ in (LO, HI)]
    n_fixed = half + 1

    def pass1(t, hh):
        is_lo = slot_is_lo[t]
        k_aug = jnp.concatenate([k_ref[0, key_rows(slot_key[t]), lanes_of(hh)], k_extra], axis=1)
        qa = qa_ref[HI, hh] if is_lo is None else qa_ref[jnp.where(is_lo, LO, HI), hh]
        s = _dot(k_aug, qa)
        if slot_causal[t]:
            s = jnp.where(causal, s, neg_inf)
        s_refs[hh][t] = s
        top = fold8(s, jnp.maximum) + bias_row(t, hh)
        if is_lo is None:
            m8[HI][hh] = jnp.maximum(m8[HI][hh], top)
        else:
            m8[LO][hh] = jnp.maximum(m8[LO][hh], jnp.where(is_lo, top, neg_inf))
            m8[HI][hh] = jnp.maximum(m8[HI][hh], jnp.where(is_lo, neg_inf, top))

    def row_max(hh):
        for qb in (LO, HI):
            m_row[qb][hh] = jnp.max(m8[qb][hh], axis=0, keepdims=True)

    def pass2(t, hh):
        shift = pick(slot_is_lo[t], m_row[LO][hh], m_row[HI][hh]) - bias_row(t, hh)
        p_refs[hh][t] = jnp.exp2(s_refs[hh][t] - shift).astype(BF16)

    ones_rows = jnp.ones((BF16_ROWS, blk), F32)
    v_rows = HEAD_DIM + BF16_ROWS

    def values_and_ones(t, hh):
        return jnp.concatenate(
            [vt_ref[0, slot_key[t], hh * HEAD_DIM:(hh + 1) * HEAD_DIM, :].astype(F32), ones_rows], axis=0)

    pv_fixed = [None] * n_h

    def value_product_fixed(hh):
        v_fixed = jnp.concatenate([values_and_ones(t, hh).astype(BF16) for t in range(n_fixed)], axis=1)
        pv_fixed[hh] = _dot(v_fixed, p_refs[hh][0:n_fixed].reshape(n_fixed * blk, blk))

    def value_product_split(hh):
        pieces = []
        for t in range(n_fixed, n_slots):
            v32 = values_and_ones(t, hh)
            pieces.append(jnp.concatenate([jnp.where(slot_is_lo[t], v32, 0.0),
                                           jnp.where(slot_is_lo[t], 0.0, v32)], axis=0).astype(BF16))
        v_split = jnp.concatenate(pieces, axis=1)
        pv_split = _dot(v_split, p_refs[hh][n_fixed:n_slots].reshape((n_slots - n_fixed) * blk, blk))
        pv_hi = pv_fixed[hh] + pv_split[v_rows:2 * v_rows]
        acc[LO][hh] = pv_split[0:HEAD_DIM]
        norm[LO][hh] = pv_split[HEAD_DIM:HEAD_DIM + 1]
        acc[HI][hh] = pv_hi[0:HEAD_DIM]
        norm[HI][hh] = pv_hi[HEAD_DIM:HEAD_DIM + 1]

    for h in heads:
        for qb in (LO, HI):
            choose_blocks(h, qb)
    for t in range(n_slots):
        pass1(t, 0)
    row_max(0)
    for h in heads:
        for t in range(n_slots):
            pass2(t, h)
            if h + 1 < n_h:
                pass1(t, h + 1)
            if t == 2 and h > 0:
                value_product_split(h - 1)
            if t == n_fixed:
                value_product_fixed(h)
        if h + 1 < n_h:
            row_max(h + 1)
    value_product_split(n_h - 1)

    for qb, o_ref, gz_ref in ((LO, o_lo_ref, gz_lo_ref), (HI, o_hi_ref, gz_hi_ref)):
        o_t = jnp.concatenate([acc[qb][hh] / norm[qb][hh] for hh in heads], axis=0)
        o_ref[0] = (o_t.T * gz_ref[0].astype(F32)).astype(BF16)


def _moba_attn(q, k, vt, kmean, gz, slopes):
    B, S, _ = q.shape
    blk = MOBA_BLOCK
    nb = S // blk
    n_h = ATTN_HEADS_PER_STEP
    width = n_h * HEAD_DIM
    half = nb // 2
    lo_blk = pl.BlockSpec((1, blk, width), lambda b, p, s: (b, s, p))
    hi_blk = pl.BlockSpec((1, blk, width), lambda b, p, s: (b, nb - 1 - s, p))
    return pl.pallas_call(
        _moba_attn_kernel,
        grid=(B, ATT_W // width, half),
        in_specs=[
            pl.BlockSpec(memory_space=pltpu.SMEM),
            lo_blk,
            hi_blk,
            pl.BlockSpec((1, S, width), lambda b, p, s: (b, 0, p)),
            pl.BlockSpec((1, nb, width, blk), lambda b, p, s: (b, 0, p, 0)),
            pl.BlockSpec((1, nb, width), lambda b, p, s: (b, 0, p)),
            lo_blk,
            hi_blk,
        ],
        out_specs=[
            lo_blk,
            pl.BlockSpec((1, blk, width), lambda b, p, s: (b, half - 1 - s, p)),
        ],
        out_shape=[jax.ShapeDtypeStruct((B, S // 2, ATT_W), BF16)] * 2,
        scratch_shapes=[
            pltpu.VMEM((2, n_h, 2 * LANES, blk), BF16),
            pltpu.VMEM((2, n_h, nb, blk), F32),
        ] + [pltpu.VMEM((nb + 1, blk, blk), F32)] * ATTN_SCRATCH_SETS
          + [pltpu.VMEM((nb + 1, blk, blk), BF16)] * ATTN_SCRATCH_SETS,
        compiler_params=pltpu.CompilerParams(
            dimension_semantics=("arbitrary", "arbitrary", "arbitrary"), vmem_limit_bytes=VMEM_LIMIT),
        name="moba_attn",
    )(slopes, q, q, k, vt, kmean, gz, gz)


def _moba_out_kernel(x_hbm, og_lo_hbm, og_hi_hbm, w_ref, o_ref, x_buf, og_buf, x_sem, og_sem,
                     *, tiles_per_seq, n_tiles):
    ts = x_buf.shape[1]
    per_half = tiles_per_seq // 2
    n = pl.program_id(0)

    def x_copy(t, slot):
        b, i = t // tiles_per_seq, t % tiles_per_seq
        return pltpu.make_async_copy(x_hbm.at[b, pl.ds(i * ts, ts), :], x_buf.at[slot], x_sem.at[slot])

    def og_copy(t, slot, second_half):
        b, i = t // tiles_per_seq, t % tiles_per_seq
        src = og_hi_hbm if second_half else og_lo_hbm
        j = jnp.maximum(i - per_half, 0) if second_half else jnp.minimum(i, per_half - 1)
        return pltpu.make_async_copy(src.at[b, pl.ds(j * ts, ts), :], og_buf.at[slot], og_sem.at[slot])

    def for_tile(t, slot, act):
        act(x_copy(t, slot))
        second_half = t % tiles_per_seq >= per_half
        if isinstance(t, int):
            act(og_copy(t, slot, second_half))
        else:
            pl.when(jnp.logical_not(second_half))(lambda: act(og_copy(t, slot, False)))
            pl.when(second_half)(lambda: act(og_copy(t, slot, True)))

    @pl.when(n == 0)
    def _():
        for t in range(OUT_RING):
            for_tile(t, t, lambda c: c.start())

    slot = n % OUT_RING
    for_tile(n, slot, lambda c: c.wait())
    o_ref[0] = x_buf[slot] + _dot(og_buf[slot], w_ref[...])

    @pl.when(n + OUT_RING < n_tiles)
    def _():
        for_tile(n + OUT_RING, slot, lambda c: c.start())


def _moba_out(x, og_lo, og_hi, w_out):
    B, S, D = x.shape
    ts = 4 * SEQ_TILE
    tiles_per_seq = S // ts
    n_tiles = B * tiles_per_seq
    assert tiles_per_seq % 2 == 0 and n_tiles >= OUT_RING
    hbm = pl.BlockSpec(memory_space=pl.ANY)
    return pl.pallas_call(
        functools.partial(_moba_out_kernel, tiles_per_seq=tiles_per_seq, n_tiles=n_tiles),
        grid=(n_tiles,),
        in_specs=[hbm, hbm, hbm, pl.BlockSpec((ATT_W, D), lambda n: (0, 0), pipeline_mode=pl.Buffered(1))],
        out_specs=pl.BlockSpec((1, ts, D), lambda n: (n // tiles_per_seq, n % tiles_per_seq, 0)),
        out_shape=jax.ShapeDtypeStruct((B, S, D), F32),
        scratch_shapes=[
            pltpu.VMEM((OUT_RING, ts, D), F32),
            pltpu.VMEM((OUT_RING, ts, ATT_W), BF16),
            pltpu.SemaphoreType.DMA((OUT_RING,)),
            pltpu.SemaphoreType.DMA((OUT_RING,)),
        ],
        compiler_params=pltpu.CompilerParams(
            dimension_semantics=("arbitrary",), vmem_limit_bytes=VMEM_LIMIT),
        name="moba_out",
    )(x, og_lo, og_hi, w_out.astype(BF16))


def kernel(x, norm0_g, w_in0, pool_w, pool_scale, conv_w, conv_b, cnorm_g, cnorm_b, w_out0,
           norm1_g, w_in1, q_norm_g, k_norm_g, w_out1):
    assert x.shape[1] % MOBA_BLOCK == 0 and x.shape[2] == D_MODEL
    x = _pool_conv_layer(x, norm0_g[0], w_in0[0], pool_w[0], pool_scale[0], conv_w[0], conv_b[0],
                         cnorm_g[0], cnorm_b[0], w_out0[0])
    q, k, vt, gz, kmean = _moba_proj(x, norm1_g[0], w_in1[0], q_norm_g[0], k_norm_g[0])
    slopes = jnp.exp2(-8.0 * jnp.arange(1, N_HEADS + 1, dtype=F32) / N_HEADS)
    og_lo, og_hi = _moba_attn(q, k, vt, kmean.reshape(kmean.shape[0], -1, ATT_W), gz, slopes)
    return _moba_out(x, og_lo, og_hi, w_out1[0])
```
